```python
import math
import jax, jax.numpy as jnp
from jax import lax
import numpy as np

D_MODEL = 1024
BATCH = 2
SEQ = 8192
DEPTH = 1

CHUNK = 64
Q_BLOCK = 128
N_MEM = 256
EPS = 1e-6
ROPE_THETA = 500000.0

SB_HEADS = 8
SB_HEAD_DIM = D_MODEL // 16
SB_WIDTH = SB_HEADS * SB_HEAD_DIM
DF_HEADS = 4
DF_QK_DIM = D_MODEL // 32
DF_V_DIM = 2 * DF_QK_DIM
DF_WIDTH = DF_HEADS * DF_V_DIM
DF_ROT_DIMS = DF_QK_DIM // 4
MEM_HEADS = 4
MEM_HEAD_DIM = D_MODEL // 16
MEM_WIDTH = MEM_HEADS * MEM_HEAD_DIM
MIX_WIDTH = SB_WIDTH + DF_WIDTH + MEM_WIDTH

PROJ_SIZES = (SB_WIDTH, SB_WIDTH, SB_WIDTH, SB_WIDTH,
              2 * DF_HEADS * DF_QK_DIM, 2 * DF_HEADS * DF_QK_DIM,
              DF_WIDTH, DF_WIDTH,
              MEM_WIDTH, MEM_WIDTH)
PROJ_WIDTH = sum(PROJ_SIZES)

kernel_name = 'hymba_stickbreak_diffattn_memxattn_block'


def _rmsnorm(x, g):
    xf = x.astype(jnp.float32)
    y = xf * lax.rsqrt(jnp.mean(xf * xf, axis=-1, keepdims=True) + EPS)
    return (y * g.astype(jnp.float32)).astype(x.dtype)


def _split_proj(p):
    parts, off = [], 0
    for n in PROJ_SIZES:
        parts.append(p[..., off:off + n])
        off += n
    return parts


def _partial_rope(x, cos, sin):
    half = DF_ROT_DIMS // 2
    x1 = x[..., :half]
    x2 = x[..., half:DF_ROT_DIMS]
    r1 = (x1 * cos - x2 * sin).astype(x.dtype)
    r2 = (x2 * cos + x1 * sin).astype(x.dtype)
    return jnp.concatenate([r1, r2, x[..., DF_ROT_DIMS:]], axis=-1)


def _to_blocks(a):
    b, s = a.shape[:2]
    a = a.reshape((b, s // Q_BLOCK, Q_BLOCK) + a.shape[2:])
    return jnp.moveaxis(a, 1, 0)


def _from_blocks(a):
    a = jnp.moveaxis(a, 0, 1)
    return a.reshape((a.shape[0], a.shape[1] * a.shape[2]) + a.shape[3:])


def _stick_breaking(q, k, v):
    s_len = q.shape[1]
    scale = SB_HEAD_DIM ** -0.5
    key_idx = jnp.arange(s_len)

    def block(args):
        qb, t0 = args
        z = jnp.einsum('bqhd,bkhd->bhqk', qb, k).astype(jnp.float32) * scale
        q_idx = t0 + jnp.arange(Q_BLOCK)
        strict = key_idx[None, :] < q_idx[:, None]
        log_1mb = jnp.where(strict, jax.nn.log_sigmoid(-z), 0.0)
        between = lax.cumsum(log_1mb, axis=3, reverse=True) - log_1mb
        w = jnp.where(strict, jnp.exp(jax.nn.log_sigmoid(z) + between), 0.0)
        return jnp.einsum('bhqk,bkhd->bqhd', w.astype(v.dtype), v)

    starts = jnp.arange(s_len // Q_BLOCK, dtype=jnp.int32) * Q_BLOCK
    return _from_blocks(lax.map(block, (_to_blocks(q), starts)))


def _diff_attention(q, k, v, lam, lambda_init, g_subln):
    s_len = q.shape[1]
    scale = DF_QK_DIM ** -0.5
    key_chunk = jnp.arange(s_len) // CHUNK

    def block(args):
        qb, t0 = args
        sc = jnp.einsum('bqhmd,bkhmd->bhmqk', qb, k).astype(jnp.float32) * scale
        q_chunk = (t0 + jnp.arange(Q_BLOCK)) // CHUNK
        allowed = key_chunk[None, :] <= q_chunk[:, None]
        p = jax.nn.softmax(jnp.where(allowed, sc, -jnp.inf), axis=-1)
        a = p[:, :, 0] - lam * p[:, :, 1]
        return jnp.einsum('bhqk,bkhe->bqhe', a.astype(v.dtype), v)

    starts = jnp.arange(s_len // Q_BLOCK, dtype=jnp.int32) * Q_BLOCK
    o = _from_blocks(lax.map(block, (_to_blocks(q), starts)))
    return _rmsnorm(o, g_subln) * (1.0 - lambda_init)


def setup_inputs(seed: int = 0) -> dict:
    key = jax.random.key(seed)
    ks = jax.random.split(key, 12)
    f32 = jnp.float32
    x = jax.random.normal(ks[0], (BATCH, SEQ, D_MODEL), f32)
    mem = jax.random.normal(ks[1], (BATCH, N_MEM, D_MODEL), f32)
    offset = jax.random.randint(ks[2], (BATCH, 1), 0, 4096, dtype=jnp.int32)
    positions = (offset + jnp.arange(SEQ, dtype=jnp.int32)[None, :]).astype(jnp.int32)
    w_in = jax.random.normal(ks[3], (DEPTH, D_MODEL, PROJ_WIDTH), f32) * D_MODEL ** -0.5
    w_mem_kv = jax.random.normal(ks[4], (DEPTH, D_MODEL, 2 * MEM_WIDTH), f32) * D_MODEL ** -0.5
    w_out = jax.random.normal(ks[5], (DEPTH, MIX_WIDTH, D_MODEL), f32) * MIX_WIDTH ** -0.5
    g_pre = 1.0 + 0.02 * jax.random.normal(ks[6], (DEPTH, D_MODEL), f32)
    g_post = 1.0 + 0.02 * jax.random.normal(ks[7], (DEPTH, D_MODEL), f32)
    g_mem = 1.0 + 0.02 * jax.random.normal(ks[8], (DEPTH, D_MODEL), f32)
    g_subln = 1.0 + 0.02 * jax.random.normal(ks[9], (DEPTH, DF_V_DIM), f32)
    df_lambda = 0.1 * jax.random.normal(ks[10], (DEPTH, 4, DF_QK_DIM), f32)
    return {'x': x, 'mem': mem, 'positions': positions, 'w_in': w_in, 'w_mem_kv': w_mem_kv,
            'w_out': w_out, 'g_pre': g_pre, 'g_post': g_post, 'g_mem': g_mem,
            'g_subln': g_subln, 'df_lambda': df_lambda}


def reference(x, mem, positions, w_in, w_mem_kv, w_out, g_pre, g_post, g_mem, g_subln, df_lambda):
    b, s_len, _ = x.shape
    n_mem = mem.shape[1]
    inv_freq = 1.0 / (ROPE_THETA ** (jnp.arange(0, DF_ROT_DIMS, 2, dtype=jnp.float32) / DF_ROT_DIMS))
    ang = positions.astype(jnp.float32)[:, :, None] * inv_freq
    cos = jnp.cos(ang)[:, :, None, None, :]
    sin = jnp.sin(ang)[:, :, None, None, :]

    for layer in range(DEPTH):
        lambda_init = 0.8 - 0.6 * math.exp(-0.3 * layer)
        h = _rmsnorm(x, g_pre[layer])
        proj = h @ w_in[layer]
        sb_q, sb_k, sb_v, sb_g, df_q, df_k, df_v, df_g, m_q, m_g = _split_proj(proj)

        shp = (b, s_len, SB_HEADS, SB_HEAD_DIM)
        y_sb = _stick_breaking(sb_q.reshape(shp), sb_k.reshape(shp), sb_v.reshape(shp))
        y_sb = y_sb.reshape(b, s_len, SB_WIDTH) * jax.nn.silu(sb_g)

        qk_shp = (b, s_len, DF_HEADS, 2, DF_QK_DIM)
        dq = _partial_rope(df_q.reshape(qk_shp), cos, sin)
        dk = _partial_rope(df_k.reshape(qk_shp), cos, sin)
        lp = df_lambda[layer].astype(jnp.float32)
        lam = jnp.exp(jnp.sum(lp[0] * lp[1])) - jnp.exp(jnp.sum(lp[2] * lp[3])) + lambda_init
        y_df = _diff_attention(dq, dk, df_v.reshape(b, s_len, DF_HEADS, DF_V_DIM),
                               lam, lambda_init, g_subln[layer])
        y_df = y_df.reshape(b, s_len, DF_WIDTH) * jax.nn.silu(df_g)

        mkv = (_rmsnorm(mem, g_mem[layer]) @ w_mem_kv[layer]).reshape(b, n_mem, 2, MEM_HEADS, MEM_HEAD_DIM)
        mq = m_q.reshape(b, s_len, MEM_HEADS, MEM_HEAD_DIM)
        sc = jnp.einsum('bshd,bmhd->bhsm', mq, mkv[:, :, 0]).astype(jnp.float32) * MEM_HEAD_DIM ** -0.5
        pm = jax.nn.softmax(sc, axis=-1).astype(mkv.dtype)
        y_m = jnp.einsum('bhsm,bmhd->bshd', pm, mkv[:, :, 1]).reshape(b, s_len, MEM_WIDTH) * jax.nn.silu(m_g)

        y = jnp.concatenate([y_sb, y_df, y_m], axis=-1) @ w_out[layer]
        x = x + _rmsnorm(y, g_post[layer])
    return x
```

```python
import functools
import math

import jax
import jax.numpy as jnp
from jax import lax
from jax.experimental import pallas as pl
from jax.experimental.pallas import tpu as pltpu

EPS = 1e-6
ROPE_THETA = 500000.0
CHUNK = 64

LANES = 128
SB_HEAD_DIM = 64
DF_QK_DIM = 32
DF_V_DIM = 64
DF_ROT_DIMS = 8
MEM_HEAD_DIM = 64
MEM_HEADS = 4

ATT_BLOCK = 256
PROJ_ROWS = 512
MERGE_ROWS = 256
PROJ_COL_CHUNK = 512
VMEM_LIMIT = 48 * 1024 * 1024

_NT = (((1,), (1,)), ((), ()))


def _rms(xf, g):
    return xf * lax.rsqrt(jnp.mean(xf * xf, axis=-1, keepdims=True) + EPS) * g


def _mem_kv_kernel(mem_ref, g_ref, w_ref, out_ref):
    h = _rms(mem_ref[0], g_ref[...]).astype(jnp.bfloat16)
    out_ref[0] = jnp.dot(h, w_ref[...], preferred_element_type=jnp.float32).astype(out_ref.dtype)


def _mem_kv(mem, g_mem, w_mem_kv):
    b, n_mem, d = mem.shape
    width = w_mem_kv.shape[1]
    return pl.pallas_call(
        _mem_kv_kernel,
        grid=(b,),
        in_specs=[pl.BlockSpec((1, n_mem, d), lambda i: (i, 0, 0)),
                  pl.BlockSpec((1, d), lambda i: (0, 0)),
                  pl.BlockSpec((d, width), lambda i: (0, 0))],
        out_specs=pl.BlockSpec((1, n_mem, width), lambda i: (i, 0, 0)),
        out_shape=jax.ShapeDtypeStruct((b, n_mem, width), jnp.bfloat16),
        name="mem_kv",
    )(mem, g_mem, w_mem_kv)


def _rope_block(p, cos_t, sin_t):
    lane = lax.broadcasted_iota(jnp.int32, p.shape, 1) % DF_QK_DIM
    half = DF_ROT_DIMS // 2
    partner = jnp.where(lane < half, pltpu.roll(p, LANES - half, 1), pltpu.roll(p, half, 1))
    return p * cos_t + partner * sin_t


def _proj_kernel(x_ref, g_ref, w_ref, cos_ref, sin_ref, out_ref, *, layout):
    h = _rms(x_ref[...], g_ref[...]).astype(jnp.bfloat16)
    n_chunks = w_ref.shape[1] // PROJ_COL_CHUNK
    for c in range(n_chunks):
        c0 = c * PROJ_COL_CHUNK
        p = jnp.dot(h, w_ref[:, c0:c0 + PROJ_COL_CHUNK], preferred_element_type=jnp.float32)
        for j in range(PROJ_COL_CHUNK // LANES):
            col = c0 + j * LANES
            blk = p[:, j * LANES:(j + 1) * LANES]
            scale, rope = layout(col)
            if rope:
                blk = _rope_block(blk, cos_ref[...], sin_ref[...])
            if scale != 1.0:
                blk = blk * scale
            out_ref[:, col:col + LANES] = blk.astype(out_ref.dtype)


def _proj(x2, g_pre, w_in, cos_t, sin_t, layout):
    n, d = x2.shape
    width = w_in.shape[1]
    return pl.pallas_call(
        functools.partial(_proj_kernel, layout=layout),
        grid=(n // PROJ_ROWS,),
        in_specs=[pl.BlockSpec((PROJ_ROWS, d), lambda i: (i, 0)),
                  pl.BlockSpec((1, d), lambda i: (0, 0)),
                  pl.BlockSpec((d, width), lambda i: (0, 0)),
                  pl.BlockSpec((PROJ_ROWS, LANES), lambda i: (i, 0)),
                  pl.BlockSpec((PROJ_ROWS, LANES), lambda i: (i, 0))],
        out_specs=pl.BlockSpec((PROJ_ROWS, width), lambda i: (i, 0)),
        out_shape=jax.ShapeDtypeStruct((n, width), jnp.bfloat16),
        compiler_params=pltpu.CompilerParams(vmem_limit_bytes=VMEM_LIMIT),
        name="proj",
    )(x2, g_pre, w_in, cos_t, sin_t)


def _neg_softplus(z):
    return -(jnp.maximum(z, 0.0) + jnp.log(1.0 + jnp.exp(-jnp.abs(z))))


def _suffix_sum(lg, tri):
    hi = lg.astype(jnp.bfloat16)
    lo = (lg - hi.astype(jnp.float32)).astype(jnp.bfloat16)
    return (jnp.dot(hi, tri, preferred_element_type=jnp.float32)
            + jnp.dot(lo, tri, preferred_element_type=jnp.float32))


def _sb_kernel(q_ref, k_ref, v_ref, out_ref):
    t = ATT_BLOCK
    qi = pl.program_id(2)
    lane = lax.broadcasted_iota(jnp.int32, (t, LANES), 1)
    head_lanes = [lane < SB_HEAD_DIM, lane >= SB_HEAD_DIM]
    q = q_ref[...]
    qs = [jnp.where(m, q, jnp.zeros_like(q)) for m in head_lanes]
    row = lax.broadcasted_iota(jnp.int32, (t, t), 0)
    col = lax.broadcasted_iota(jnp.int32, (t, t), 1)
    tri = (row >= col).astype(jnp.bfloat16)
    strict = col < row

    def block(kb, carries, masked):
        start = pl.multiple_of(kb * t, t)
        k = k_ref[pl.ds(start, t), :]
        v = v_ref[pl.ds(start, t), :]
        acc = carries[0]
        new_carries = []
        for h in range(2):
            z = lax.dot_general(qs[h], k, _NT, preferred_element_type=jnp.float32)
            lg = _neg_softplus(z)
            if masked:
                lg = jnp.where(strict, lg, 0.0)
            local = _suffix_sum(lg, tri)
            c = local + jnp.concatenate([carries[1 + h]] * (t // LANES), axis=1)
            w = jnp.exp(z + c)
            if masked:
                w = jnp.where(strict, w, 0.0)
            vh = jnp.where(head_lanes[h], v, jnp.zeros_like(v))
            acc = acc + jnp.dot(w.astype(jnp.bfloat16), vh, preferred_element_type=jnp.float32)
            new_carries.append(carries[1 + h] + jnp.broadcast_to(local[:, :1], (t, LANES)))
        return (acc, new_carries[0], new_carries[1])

    zeros = jnp.zeros((t, LANES), jnp.float32)
    state = block(qi, (zeros, zeros, zeros), True)
    state = lax.fori_loop(0, qi, lambda j, s: block(qi - 1 - j, s, False), state)
    out_ref[...] = state[0]


def _sb_attn(proj, batch, seq, q_col, k_col, v_col, n_pairs):
    t = ATT_BLOCK
    nq = seq // t
    return pl.pallas_call(
        _sb_kernel,
        grid=(batch, n_pairs, nq),
        in_specs=[pl.BlockSpec((t, LANES), lambda b, p, i: (b * nq + i, q_col + p)),
                  pl.BlockSpec((seq, LANES), lambda b, p, i: (b, k_col + p)),
                  pl.BlockSpec((seq, LANES), lambda b, p, i: (b, v_col + p))],
        out_specs=pl.BlockSpec((t, LANES), lambda b, p, i: (b * nq + i, p)),
        out_shape=jax.ShapeDtypeStruct((batch * seq, n_pairs * LANES), jnp.float32),
        compiler_params=pltpu.CompilerParams(vmem_limit_bytes=VMEM_LIMIT),
        name="sb_attn",
    )(proj, proj, proj)


def _df_kernel(q_ref, k_ref, v_ref, lam_ref, g_ref, out_ref, *, lambda_init):
    t = ATT_BLOCK
    qi = pl.program_id(2)
    lane = lax.broadcasted_iota(jnp.int32, (t, LANES), 1)
    q = q_ref[...]
    qs = [jnp.where(lane // DF_QK_DIM == s, q, jnp.zeros_like(q)) for s in range(4)]
    head_lanes = [lane < DF_V_DIM, lane >= DF_V_DIM]
    row = lax.broadcasted_iota(jnp.int32, (t, t), 0)
    col = lax.broadcasted_iota(jnp.int32, (t, t), 1)
    allowed = (col // CHUNK) <= (row // CHUNK)

    def block(kb, state, masked):
        start = pl.multiple_of(kb * t, t)
        k = k_ref[pl.ds(start, t), :]
        v = v_ref[pl.ds(start, t), :]
        vh = [jnp.where(m, v, jnp.ones_like(v)) for m in head_lanes]
        new_state = []
        for s in range(4):
            m_run, acc = state[s]
            sc = lax.dot_general(qs[s], k, _NT, preferred_element_type=jnp.float32)
            if masked:
                sc = jnp.where(allowed, sc, -jnp.inf)
            m_blk = jnp.broadcast_to(jnp.max(sc, axis=1, keepdims=True), (t, LANES))
            m_new = jnp.maximum(m_run, m_blk)
            p = jnp.exp(sc - jnp.concatenate([m_new] * (t // LANES), axis=1))
            alpha = jnp.exp(m_run - m_new)
            acc = acc * alpha + jnp.dot(p.astype(jnp.bfloat16), vh[s // 2],
                                        preferred_element_type=jnp.float32)
            new_state.append((m_new, acc))
        return tuple(new_state)

    neg = jnp.full((t, LANES), -jnp.inf, jnp.float32)
    zeros = jnp.zeros((t, LANES), jnp.float32)
    state = block(qi, tuple((neg, zeros) for _ in range(4)), True)
    state = lax.fori_loop(0, qi, lambda j, s: block(qi - 1 - j, s, False), state)

    lp = lam_ref[...]
    lam = (jnp.exp(jnp.sum(lp[0:1] * lp[1:2], axis=1, keepdims=True))
           - jnp.exp(jnp.sum(lp[2:3] * lp[3:4], axis=1, keepdims=True)) + lambda_init)
    outs = []
    for h in range(2):
        o = []
        for m in range(2):
            acc = state[2 * h + m][1]
            denom = pltpu.roll(acc, DF_V_DIM, 1)
            o.append(acc / denom)
        outs.append(o[0] - lam * o[1])
    o = jnp.where(head_lanes[0], outs[0], outs[1])
    sq = o * o
    ms = [jnp.sum(jnp.where(m, sq, 0.0), axis=1, keepdims=True) * (1.0 / DF_V_DIM) for m in head_lanes]
    ms = jnp.where(head_lanes[0], ms[0], ms[1])
    out_ref[...] = o * lax.rsqrt(ms + EPS) * g_ref[...] * (1.0 - lambda_init)


def _df_attn(proj, df_lambda, g_sub2, batch, seq, q_col, k_col, v_col, n_pairs, lambda_init):
    t = ATT_BLOCK
    nq = seq // t
    return pl.pallas_call(
        functools.partial(_df_kernel, lambda_init=lambda_init),
        grid=(batch, n_pairs, nq),
        in_specs=[pl.BlockSpec((t, LANES), lambda b, p, i: (b * nq + i, q_col + p)),
                  pl.BlockSpec((seq, LANES), lambda b, p, i: (b, k_col + p)),
                  pl.BlockSpec((seq, LANES), lambda b, p, i: (b, v_col + p)),
                  pl.BlockSpec(df_lambda.shape, lambda b, p, i: (0, 0)),
                  pl.BlockSpec((1, LANES), lambda b, p, i: (0, 0))],
        out_specs=pl.BlockSpec((t, LANES), lambda b, p, i: (b * nq + i, p)),
        out_shape=jax.ShapeDtypeStruct((batch * seq, n_pairs * LANES), jnp.float32),
        compiler_params=pltpu.CompilerParams(vmem_limit_bytes=VMEM_LIMIT),
        name="df_attn",
    )(proj, proj, proj, df_lambda, g_sub2)


def _silu(g):
    return g * (1.0 / (1.0 + jnp.exp(-g)))


def _merge_kernel(x_ref, ysb_ref, ydf_ref, gsb_ref, gdf_ref, mq_ref, gm_ref, mkv_ref,
                  w_ref, g_ref, out_ref):
    mq = mq_ref[...]
    width = mq.shape[1]
    kv = mkv_ref[0]
    km = kv[:, :width]
    vm = kv[:, width:]
    lane_q = lax.broadcasted_iota(jnp.int32, mq.shape, 1) // MEM_HEAD_DIM
    lane_v = lax.broadcasted_iota(jnp.int32, vm.shape, 1) // MEM_HEAD_DIM
    y_m = jnp.zeros(mq.shape, jnp.float32)
    for h in range(MEM_HEADS):
        qh = jnp.where(lane_q == h, mq, jnp.zeros_like(mq))
        sc = lax.dot_general(qh, km, _NT, preferred_element_type=jnp.float32)
        p = jnp.exp(sc - jnp.max(sc, axis=1, keepdims=True))
        p = p / jnp.sum(p, axis=1, keepdims=True)
        vh = jnp.where(lane_v == h, vm, jnp.zeros_like(vm))
        y_m = y_m + jnp.dot(p.astype(jnp.bfloat16), vh, preferred_element_type=jnp.float32)
    y = jnp.concatenate([ysb_ref[...] * _silu(gsb_ref[...].astype(jnp.float32)),
                         ydf_ref[...] * _silu(gdf_ref[...].astype(jnp.float32)),
                         y_m * _silu(gm_ref[...].astype(jnp.float32))], axis=1)
    o = jnp.dot(y.astype(jnp.bfloat16), w_ref[...], preferred_element_type=jnp.float32)
    out_ref[...] = x_ref[...] + _rms(o, g_ref[...])


def _merge(x2, y_sb, y_df, proj, mkv, w_out, g_post, seq, cols):
    n, d = x2.shape
    r = MERGE_ROWS
    sb_w, df_w, m_w = y_sb.shape[1], y_df.shape[1], mkv.shape[2] // 2
    gsb_col, gdf_col, mq_col, gm_col = cols
    per_batch = seq // r
    return pl.pallas_call(
        _merge_kernel,
        grid=(n // r,),
        in_specs=[pl.BlockSpec((r, d), lambda i: (i, 0)),
                  pl.BlockSpec((r, sb_w), lambda i: (i, 0)),
                  pl.BlockSpec((r, df_w), lambda i: (i, 0)),
                  pl.BlockSpec((r, sb_w), lambda i: (i, gsb_col // sb_w)),
                  pl.BlockSpec((r, df_w), lambda i: (i, gdf_col // df_w)),
                  pl.BlockSpec((r, m_w), lambda i: (i, mq_col // m_w)),
                  pl.BlockSpec((r, m_w), lambda i: (i, gm_col // m_w)),
                  pl.BlockSpec((1,) + mkv.shape[1:], lambda i: (i // per_batch, 0, 0)),
                  pl.BlockSpec(w_out.shape, lambda i: (0, 0)),
                  pl.BlockSpec((1, d), lambda i: (0, 0))],
        out_specs=pl.BlockSpec((r, d), lambda i: (i, 0)),
        out_shape=jax.ShapeDtypeStruct((n, d), jnp.float32),
        compiler_params=pltpu.CompilerParams(vmem_limit_bytes=VMEM_LIMIT),
        name="merge",
    )(x2, y_sb, y_df, proj, proj, proj, proj, mkv, w_out, g_post)


def _rope_tables(positions):
    half = DF_ROT_DIMS // 2
    inv_freq = 1.0 / (ROPE_THETA ** (jnp.arange(0, DF_ROT_DIMS, 2, dtype=jnp.float32) / DF_ROT_DIMS))
    ang = positions.astype(jnp.float32)[:, :, None] * inv_freq
    cos, sin = jnp.cos(ang), jnp.sin(ang)
    pad = DF_QK_DIM - DF_ROT_DIMS
    cos_g = jnp.concatenate([cos, cos, jnp.ones(cos.shape[:2] + (pad,), jnp.float32)], axis=-1)
    sin_g = jnp.concatenate([-sin, sin, jnp.zeros(sin.shape[:2] + (pad,), jnp.float32)], axis=-1)
    reps = LANES // DF_QK_DIM
    b, s = positions.shape
    return (jnp.tile(cos_g, (1, 1, reps)).reshape(b * s, LANES),
            jnp.tile(sin_g, (1, 1, reps)).reshape(b * s, LANES))


def kernel(x, mem, positions, w_in, w_mem_kv, w_out, g_pre, g_post, g_mem, g_subln, df_lambda):
    batch, seq, d = x.shape
    depth = w_in.shape[0]
    sb_w, df_w, m_w = d // 2, d // 4, d // 4
    sb_q, sb_k, sb_v, sb_g = 0, sb_w, 2 * sb_w, 3 * sb_w
    df_q = 4 * sb_w
    df_k, df_v, df_g = df_q + df_w, df_q + 2 * df_w, df_q + 3 * df_w
    m_q, m_g = df_q + 4 * df_w, df_q + 4 * df_w + m_w
    assert m_g + m_w == w_in.shape[2]
    assert seq % ATT_BLOCK == 0 and (batch * seq) % PROJ_ROWS == 0 and seq % MERGE_ROWS == 0

    def layout(col):
        if sb_q <= col < sb_k:
            return SB_HEAD_DIM ** -0.5, False
        if df_q <= col < df_k:
            return DF_QK_DIM ** -0.5, True
        if df_k <= col < df_v:
            return 1.0, True
        if m_q <= col < m_g:
            return MEM_HEAD_DIM ** -0.5, False
        return 1.0, False

    cos_t, sin_t = _rope_tables(positions)
    x2 = x.reshape(batch * seq, d)
    for layer in range(depth):
        lambda_init = 0.8 - 0.6 * math.exp(-0.3 * layer)
        mkv = _mem_kv(mem, g_mem[layer][None], w_mem_kv[layer].astype(jnp.bfloat16))
        proj = _proj(x2, g_pre[layer][None], w_in[layer].astype(jnp.bfloat16), cos_t, sin_t, layout)
        y_sb = _sb_attn(proj, batch, seq, sb_q // LANES, sb_k // LANES, sb_v // LANES, sb_w // LANES)
        g_sub2 = jnp.tile(g_subln[layer], LANES // DF_V_DIM)[None]
        y_df = _df_attn(proj, df_lambda[layer], g_sub2, batch, seq,
                        df_q // LANES, df_k // LANES, df_v // LANES, df_w // LANES, lambda_init)
        x2 = _merge(x2, y_sb, y_df, proj, mkv, w_out[layer].astype(jnp.bfloat16), g_post[layer][None],
                    seq, (sb_g, df_g, m_q, m_g))
    return x2.reshape(batch, seq, d)
```

```python
import functools
import math

import jax
import jax.numpy as jnp
from jax import lax
from jax.experimental import pallas as pl
from jax.experimental.pallas import tpu as pltpu

EPS = 1e-6
ROPE_THETA = 500000.0
CHUNK = 64

LANES = 128
SB_HEAD_DIM = 64
DF_QK_DIM = 32
DF_V_DIM = 64
DF_ROT_DIMS = 8
MEM_HEAD_DIM = 64
MEM_HEADS = 4

ATT_BLOCK = 256
PROJ_ROWS = 512
MERGE_ROWS = 256
PROJ_COL_CHUNK = 512
VMEM_LIMIT = 48 * 1024 * 1024
SB_DEAD_LOG = -110.0

_NT = (((1,), (1,)), ((), ()))


def _rms(xf, g):
    return xf * lax.rsqrt(jnp.mean(xf * xf, axis=-1, keepdims=True) + EPS) * g


def _mem_kv_kernel(mem_ref, g_ref, w_ref, out_ref):
    h = _rms(mem_ref[0], g_ref[...]).astype(jnp.bfloat16)
    out_ref[0] = jnp.dot(h, w_ref[...], preferred_element_type=jnp.float32).astype(out_ref.dtype)


def _mem_kv(mem, g_mem, w_mem_kv):
    b, n_mem, d = mem.shape
    width = w_mem_kv.shape[1]
    return pl.pallas_call(
        _mem_kv_kernel,
        grid=(b,),
        in_specs=[pl.BlockSpec((1, n_mem, d), lambda i: (i, 0, 0)),
                  pl.BlockSpec((1, d), lambda i: (0, 0)),
                  pl.BlockSpec((d, width), lambda i: (0, 0))],
        out_specs=pl.BlockSpec((1, n_mem, width), lambda i: (i, 0, 0)),
        out_shape=jax.ShapeDtypeStruct((b, n_mem, width), jnp.bfloat16),
        name="mem_kv",
    )(mem, g_mem, w_mem_kv)


def _rope_block(p, cos_t, sin_t):
    lane = lax.broadcasted_iota(jnp.int32, p.shape, 1) % DF_QK_DIM
    half = DF_ROT_DIMS // 2
    partner = jnp.where(lane < half, pltpu.roll(p, LANES - half, 1), pltpu.roll(p, half, 1))
    return p * cos_t + partner * sin_t


def _proj_kernel(x_ref, g_ref, w_ref, cos_ref, sin_ref, out_ref, *, layout):
    h = _rms(x_ref[...], g_ref[...]).astype(jnp.bfloat16)
    n_chunks = w_ref.shape[1] // PROJ_COL_CHUNK
    for c in range(n_chunks):
        c0 = c * PROJ_COL_CHUNK
        p = jnp.dot(h, w_ref[:, c0:c0 + PROJ_COL_CHUNK], preferred_element_type=jnp.float32)
        for j in range(PROJ_COL_CHUNK // LANES):
            col = c0 + j * LANES
            blk = p[:, j * LANES:(j + 1) * LANES]
            scale, rope = layout(col)
            if rope:
                blk = _rope_block(blk, cos_ref[...], sin_ref[...])
            if scale != 1.0:
                blk = blk * scale
            out_ref[:, col:col + LANES] = blk.astype(out_ref.dtype)


def _proj(x2, g_pre, w_in, cos_t, sin_t, layout):
    n, d = x2.shape
    width = w_in.shape[1]
    return pl.pallas_call(
        functools.partial(_proj_kernel, layout=layout),
        grid=(n // PROJ_ROWS,),
        in_specs=[pl.BlockSpec((PROJ_ROWS, d), lambda i: (i, 0)),
                  pl.BlockSpec((1, d), lambda i: (0, 0)),
                  pl.BlockSpec((d, width), lambda i: (0, 0)),
                  pl.BlockSpec((PROJ_ROWS, LANES), lambda i: (i, 0)),
                  pl.BlockSpec((PROJ_ROWS, LANES), lambda i: (i, 0))],
        out_specs=pl.BlockSpec((PROJ_ROWS, width), lambda i: (i, 0)),
        out_shape=jax.ShapeDtypeStruct((n, width), jnp.bfloat16),
        compiler_params=pltpu.CompilerParams(vmem_limit_bytes=VMEM_LIMIT),
        name="proj",
    )(x2, g_pre, w_in, cos_t, sin_t)


def _neg_softplus(z):
    return -(jnp.maximum(z, 0.0) + jnp.log(1.0 + jnp.exp(-jnp.abs(z))))


def _suffix_sum(lg, tri):
    hi = lg.astype(jnp.bfloat16)
    lo = (lg - hi.astype(jnp.float32)).astype(jnp.bfloat16)
    return (jnp.dot(hi, tri, preferred_element_type=jnp.float32)
            + jnp.dot(lo, tri, preferred_element_type=jnp.float32))


def _sb_kernel(q_ref, k_ref, v_ref, out_ref):
    t = ATT_BLOCK
    qi = pl.program_id(2)
    lane = lax.broadcasted_iota(jnp.int32, (t, LANES), 1)
    head_lanes = [lane < SB_HEAD_DIM, lane >= SB_HEAD_DIM]
    q = q_ref[...]
    qs = [jnp.where(m, q, jnp.zeros_like(q)) for m in head_lanes]
    row = lax.broadcasted_iota(jnp.int32, (t, t), 0)
    col = lax.broadcasted_iota(jnp.int32, (t, t), 1)
    tri = (row >= col).astype(jnp.bfloat16)
    strict = col < row

    def block(kb, carries, masked):
        start = pl.multiple_of(kb * t, t)
        k = k_ref[pl.ds(start, t), :]
        v = v_ref[pl.ds(start, t), :]
        acc = carries[0]
        new_carries = []
        for h in range(2):
            z = lax.dot_general(qs[h], k, _NT, preferred_element_type=jnp.float32)
            lg = _neg_softplus(z)
            if masked:
                lg = jnp.where(strict, lg, 0.0)
            local = _suffix_sum(lg, tri)
            c = local + jnp.concatenate([carries[1 + h]] * (t // LANES), axis=1)
            w = jnp.exp(z + c)
            if masked:
                w = jnp.where(strict, w, 0.0)
            vh = jnp.where(head_lanes[h], v, jnp.zeros_like(v))
            acc = acc + jnp.dot(w.astype(jnp.bfloat16), vh, preferred_element_type=jnp.float32)
            new_carries.append(carries[1 + h] + jnp.broadcast_to(local[:, :1], (t, LANES)))
        return (acc, new_carries[0], new_carries[1])

    def live(carries):
        return (jnp.max(jnp.maximum(carries[1], carries[2])) > SB_DEAD_LOG).astype(jnp.int32)

    def step(loop_state):
        kb, _, carries = loop_state
        carries = block(kb, carries, False)
        return kb - 1, live(carries), carries

    zeros = jnp.zeros((t, LANES), jnp.float32)
    carries = block(qi, (zeros, zeros, zeros), True)
    _, _, carries = lax.while_loop(lambda s: (s[0] >= 0) & (s[1] > 0), step,
                                   (qi - 1, live(carries), carries))
    out_ref[...] = carries[0]


def _sb_attn(proj, batch, seq, q_col, k_col, v_col, n_pairs):
    t = ATT_BLOCK
    nq = seq // t
    return pl.pallas_call(
        _sb_kernel,
        grid=(batch, n_pairs, nq),
        in_specs=[pl.BlockSpec((t, LANES), lambda b, p, i: (b * nq + i, q_col + p)),
                  pl.BlockSpec((seq, LANES), lambda b, p, i: (b, k_col + p)),
                  pl.BlockSpec((seq, LANES), lambda b, p, i: (b, v_col + p))],
        out_specs=pl.BlockSpec((t, LANES), lambda b, p, i: (b * nq + i, p)),
        out_shape=jax.ShapeDtypeStruct((batch * seq, n_pairs * LANES), jnp.float32),
        compiler_params=pltpu.CompilerParams(vmem_limit_bytes=VMEM_LIMIT),
        name="sb_attn",
    )(proj, proj, proj)


def _df_kernel(q_ref, k_ref, v_ref, lam_ref, g_ref, out_ref, *, lambda_init):
    t = ATT_BLOCK
    qi = pl.program_id(2)
    lane = lax.broadcasted_iota(jnp.int32, (t, LANES), 1)
    q = q_ref[...]
    qs = [jnp.where(lane // DF_QK_DIM == s, q, jnp.zeros_like(q)) for s in range(4)]
    head_lanes = [lane < DF_V_DIM, lane >= DF_V_DIM]
    row = lax.broadcasted_iota(jnp.int32, (t, t), 0)
    col = lax.broadcasted_iota(jnp.int32, (t, t), 1)
    allowed = (col // CHUNK) <= (row // CHUNK)

    def block(kb, state, masked):
        start = pl.multiple_of(kb * t, t)
        k = k_ref[pl.ds(start, t), :]
        v = v_ref[pl.ds(start, t), :]
        vh = [jnp.where(m, v, jnp.ones_like(v)) for m in head_lanes]
        new_state = []
        for s in range(4):
            m_run, acc = state[s]
            sc = lax.dot_general(qs[s], k, _NT, preferred_element_type=jnp.float32)
            if masked:
                sc = jnp.where(allowed, sc, -jnp.inf)
            m_blk = jnp.broadcast_to(jnp.max(sc, axis=1, keepdims=True), (t, LANES))
            m_new = jnp.maximum(m_run, m_blk)
            p = jnp.exp(sc - jnp.concatenate([m_new] * (t // LANES), axis=1))
            alpha = jnp.exp(m_run - m_new)
            acc = acc * alpha + jnp.dot(p.astype(jnp.bfloat16), vh[s // 2],
                                        preferred_element_type=jnp.float32)
            new_state.append((m_new, acc))
        return tuple(new_state)

    neg = jnp.full((t, LANES), -jnp.inf, jnp.float32)
    zeros = jnp.zeros((t, LANES), jnp.float32)
    state = block(qi, tuple((neg, zeros) for _ in range(4)), True)
    state = lax.fori_loop(0, qi, lambda j, s: block(qi - 1 - j, s, False), state)

    lp = lam_ref[...]
    lam = (jnp.exp(jnp.sum(lp[0:1] * lp[1:2], axis=1, keepdims=True))
           - jnp.exp(jnp.sum(lp[2:3] * lp[3:4], axis=1, keepdims=True)) + lambda_init)
    outs = []
    for h in range(2):
        o = []
        for m in range(2):
            acc = state[2 * h + m][1]
            denom = pltpu.roll(acc, DF_V_DIM, 1)
            o.append(acc / denom)
        outs.append(o[0] - lam * o[1])
    o = jnp.where(head_lanes[0], outs[0], outs[1])
    sq = o * o
    ms = [jnp.sum(jnp.where(m, sq, 0.0), axis=1, keepdims=True) * (1.0 / DF_V_DIM) for m in head_lanes]
    ms = jnp.where(head_lanes[0], ms[0], ms[1])
    out_ref[...] = o * lax.rsqrt(ms + EPS) * g_ref[...] * (1.0 - lambda_init)


def _df_attn(proj, df_lambda, g_sub2, batch, seq, q_col, k_col, v_col, n_pairs, lambda_init):
    t = ATT_BLOCK
    nq = seq // t
    return pl.pallas_call(
        functools.partial(_df_kernel, lambda_init=lambda_init),
        grid=(batch, n_pairs, nq),
        in_specs=[pl.BlockSpec((t, LANES), lambda b, p, i: (b * nq + i, q_col + p)),
                  pl.BlockSpec((seq, LANES), lambda b, p, i: (b, k_col + p)),
                  pl.BlockSpec((seq, LANES), lambda b, p, i: (b, v_col + p)),
                  pl.BlockSpec(df_lambda.shape, lambda b, p, i: (0, 0)),
                  pl.BlockSpec((1, LANES), lambda b, p, i: (0, 0))],
        out_specs=pl.BlockSpec((t, LANES), lambda b, p, i: (b * nq + i, p)),
        out_shape=jax.ShapeDtypeStruct((batch * seq, n_pairs * LANES), jnp.float32),
        compiler_params=pltpu.CompilerParams(vmem_limit_bytes=VMEM_LIMIT),
        name="df_attn",
    )(proj, proj, proj, df_lambda, g_sub2)


def _silu(g):
    return g * (1.0 / (1.0 + jnp.exp(-g)))


def _merge_kernel(x_ref, ysb_ref, ydf_ref, gsb_ref, gdf_ref, mq_ref, gm_ref, mkv_ref,
                  w_ref, g_ref, out_ref):
    mq = mq_ref[...]
    width = mq.shape[1]
    kv = mkv_ref[0]
    km = kv[:, :width]
    vm = kv[:, width:]
    lane_q = lax.broadcasted_iota(jnp.int32, mq.shape, 1) // MEM_HEAD_DIM
    lane_v = lax.broadcasted_iota(jnp.int32, vm.shape, 1) // MEM_HEAD_DIM
    y_m = jnp.zeros(mq.shape, jnp.float32)
    for h in range(MEM_HEADS):
        qh = jnp.where(lane_q == h, mq, jnp.zeros_like(mq))
        sc = lax.dot_general(qh, km, _NT, preferred_element_type=jnp.float32)
        p = jnp.exp(sc - jnp.max(sc, axis=1, keepdims=True))
        p = p / jnp.sum(p, axis=1, keepdims=True)
        vh = jnp.where(lane_v == h, vm, jnp.zeros_like(vm))
        y_m = y_m + jnp.dot(p.astype(jnp.bfloat16), vh, preferred_element_type=jnp.float32)
    y = jnp.concatenate([ysb_ref[...] * _silu(gsb_ref[...].astype(jnp.float32)),
                         ydf_ref[...] * _silu(gdf_ref[...].astype(jnp.float32)),
                         y_m * _silu(gm_ref[...].astype(jnp.float32))], axis=1)
    o = jnp.dot(y.astype(jnp.bfloat16), w_ref[...], preferred_element_type=jnp.float32)
    out_ref[...] = x_ref[...] + _rms(o, g_ref[...])


def _merge(x2, y_sb, y_df, proj, mkv, w_out, g_post, seq, cols):
    n, d = x2.shape
    r = MERGE_ROWS
    sb_w, df_w, m_w = y_sb.shape[1], y_df.shape[1], mkv.shape[2] // 2
    gsb_col, gdf_col, mq_col, gm_col = cols
    per_batch = seq // r
    return pl.pallas_call(
        _merge_kernel,
        grid=(n // r,),
        in_specs=[pl.BlockSpec((r, d), lambda i: (i, 0)),
                  pl.BlockSpec((r, sb_w), lambda i: (i, 0)),
                  pl.BlockSpec((r, df_w), lambda i: (i, 0)),
                  pl.BlockSpec((r, sb_w), lambda i: (i, gsb_col // sb_w)),
                  pl.BlockSpec((r, df_w), lambda i: (i, gdf_col // df_w)),
                  pl.BlockSpec((r, m_w), lambda i: (i, mq_col // m_w)),
                  pl.BlockSpec((r, m_w), lambda i: (i, gm_col // m_w)),
                  pl.BlockSpec((1,) + mkv.shape[1:], lambda i: (i // per_batch, 0, 0)),
                  pl.BlockSpec(w_out.shape, lambda i: (0, 0)),
                  pl.BlockSpec((1, d), lambda i: (0, 0))],
        out_specs=pl.BlockSpec((r, d), lambda i: (i, 0)),
        out_shape=jax.ShapeDtypeStruct((n, d), jnp.float32),
        compiler_params=pltpu.CompilerParams(vmem_limit_bytes=VMEM_LIMIT),
        name="merge",
    )(x2, y_sb, y_df, proj, proj, proj, proj, mkv, w_out, g_post)


def _rope_tables(positions):
    half = DF_ROT_DIMS // 2
    inv_freq = 1.0 / (ROPE_THETA ** (jnp.arange(0, DF_ROT_DIMS, 2, dtype=jnp.float32) / DF_ROT_DIMS))
    ang = positions.astype(jnp.float32)[:, :, None] * inv_freq
    cos, sin = jnp.cos(ang), jnp.sin(ang)
    pad = DF_QK_DIM - DF_ROT_DIMS
    cos_g = jnp.concatenate([cos, cos, jnp.ones(cos.shape[:2] + (pad,), jnp.float32)], axis=-1)
    sin_g = jnp.concatenate([-sin, sin, jnp.zeros(sin.shape[:2] + (pad,), jnp.float32)], axis=-1)
    reps = LANES // DF_QK_DIM
    b, s = positions.shape
    return (jnp.tile(cos_g, (1, 1, reps)).reshape(b * s, LANES),
            jnp.tile(sin_g, (1, 1, reps)).reshape(b * s, LANES))


def kernel(x, mem, positions, w_in, w_mem_kv, w_out, g_pre, g_post, g_mem, g_subln, df_lambda):
    batch, seq, d = x.shape
    depth = w_in.shape[0]
    sb_w, df_w, m_w = d // 2, d // 4, d // 4
    sb_q, sb_k, sb_v, sb_g = 0, sb_w, 2 * sb_w, 3 * sb_w
    df_q = 4 * sb_w
    df_k, df_v, df_g = df_q + df_w, df_q + 2 * df_w, df_q + 3 * df_w
    m_q, m_g = df_q + 4 * df_w, df_q + 4 * df_w + m_w
    assert m_g + m_w == w_in.shape[2]
    assert seq % ATT_BLOCK == 0 and (batch * seq) % PROJ_ROWS == 0 and seq % MERGE_ROWS == 0

    def layout(col):
        if sb_q <= col < sb_k:
            return SB_HEAD_DIM ** -0.5, False
        if df_q <= col < df_k:
            return DF_QK_DIM ** -0.5, True
        if df_k <= col < df_v:
            return 1.0, True
        if m_q <= col < m_g:
            return MEM_HEAD_DIM ** -0.5, False
        return 1.0, False

    cos_t, sin_t = _rope_tables(positions)
    x2 = x.reshape(batch * seq, d)
    for layer in range(depth):
        lambda_init = 0.8 - 0.6 * math.exp(-0.3 * layer)
        mkv = _mem_kv(mem, g_mem[layer][None], w_mem_kv[layer].astype(jnp.bfloat16))
        proj = _proj(x2, g_pre[layer][None], w_in[layer].astype(jnp.bfloat16), cos_t, sin_t, layout)
        y_sb = _sb_attn(proj, batch, seq, sb_q // LANES, sb_k // LANES, sb_v // LANES, sb_w // LANES)
        g_sub2 = jnp.tile(g_subln[layer], LANES // DF_V_DIM)[None]
        y_df = _df_attn(proj, df_lambda[layer], g_sub2, batch, seq,
                        df_q // LANES, df_k // LANES, df_v // LANES, df_w // LANES, lambda_init)
        x2 = _merge(x2, y_sb, y_df, proj, mkv, w_out[layer].astype(jnp.bfloat16), g_post[layer][None],
                    seq, (sb_g, df_g, m_q, m_g))
    return x2.reshape(batch, seq, d)
```

```python
import functools
import math

import jax
import jax.numpy as jnp
from jax import lax
from jax.experimental import pallas as pl
from jax.experimental.pallas import tpu as pltpu

EPS = 1e-6
ROPE_THETA = 500000.0
CHUNK = 64

LANES = 128
SB_HEAD_DIM = 64
DF_QK_DIM = 32
DF_V_DIM = 64
DF_ROT_DIMS = 8
MEM_HEAD_DIM = 64
MEM_HEADS = 4
DF_ONES_ROWS = 16
LOG2E = 1.4426950408889634

ATT_BLOCK = 256
PROJ_ROWS = 512
MERGE_ROWS = 256
PROJ_COL_CHUNK = 512
VMEM_LIMIT = 48 * 1024 * 1024
SB_DEAD_LOG = -110.0

_NT = (((1,), (1,)), ((), ()))


def _rms(xf, g):
    return xf * lax.rsqrt(jnp.mean(xf * xf, axis=-1, keepdims=True) + EPS) * g


def _mem_kv_kernel(mem_ref, g_ref, w_ref, out_ref):
    h = _rms(mem_ref[0], g_ref[...]).astype(jnp.bfloat16)
    out_ref[0] = jnp.dot(h, w_ref[...], preferred_element_type=jnp.float32).astype(out_ref.dtype)


def _mem_kv(mem, g_mem, w_mem_kv):
    b, n_mem, d = mem.shape
    width = w_mem_kv.shape[1]
    return pl.pallas_call(
        _mem_kv_kernel,
        grid=(b,),
        in_specs=[pl.BlockSpec((1, n_mem, d), lambda i: (i, 0, 0)),
                  pl.BlockSpec((1, d), lambda i: (0, 0)),
                  pl.BlockSpec((d, width), lambda i: (0, 0))],
        out_specs=pl.BlockSpec((1, n_mem, width), lambda i: (i, 0, 0)),
        out_shape=jax.ShapeDtypeStruct((b, n_mem, width), jnp.bfloat16),
        name="mem_kv",
    )(mem, g_mem, w_mem_kv)


def _rope_block(p, cos_t, sin_t):
    lane = lax.broadcasted_iota(jnp.int32, p.shape, 1) % DF_QK_DIM
    half = DF_ROT_DIMS // 2
    partner = jnp.where(lane < half, pltpu.roll(p, LANES - half, 1), pltpu.roll(p, half, 1))
    return p * cos_t + partner * sin_t


def _proj_kernel(x_ref, g_ref, w_ref, wvt_ref, cos_ref, sin_ref, out_ref, vt_ref, *, layout):
    h = _rms(x_ref[...], g_ref[...]).astype(jnp.bfloat16)
    n_chunks = w_ref.shape[1] // PROJ_COL_CHUNK
    for c in range(n_chunks):
        c0 = c * PROJ_COL_CHUNK
        p = jnp.dot(h, w_ref[:, c0:c0 + PROJ_COL_CHUNK], preferred_element_type=jnp.float32)
        for j in range(PROJ_COL_CHUNK // LANES):
            col = c0 + j * LANES
            blk = p[:, j * LANES:(j + 1) * LANES]
            scale, rope = layout(col)
            if rope:
                blk = _rope_block(blk, cos_ref[...], sin_ref[...])
            if scale != 1.0:
                blk = blk * scale
            out_ref[:, col:col + LANES] = blk.astype(out_ref.dtype)
    vt_ref[...] = lax.dot_general(wvt_ref[...], h, _NT,
                                  preferred_element_type=jnp.float32).astype(vt_ref.dtype)


def _proj(x2, g_pre, w_in, w_vt, cos_t, sin_t, layout):
    n, d = x2.shape
    width = w_in.shape[1]
    v_width = w_vt.shape[0]
    return pl.pallas_call(
        functools.partial(_proj_kernel, layout=layout),
        grid=(n // PROJ_ROWS,),
        in_specs=[pl.BlockSpec((PROJ_ROWS, d), lambda i: (i, 0)),
                  pl.BlockSpec((1, d), lambda i: (0, 0)),
                  pl.BlockSpec((d, width), lambda i: (0, 0)),
                  pl.BlockSpec((v_width, d), lambda i: (0, 0)),
                  pl.BlockSpec((PROJ_ROWS, LANES), lambda i: (i, 0)),
                  pl.BlockSpec((PROJ_ROWS, LANES), lambda i: (i, 0))],
        out_specs=[pl.BlockSpec((PROJ_ROWS, width), lambda i: (i, 0)),
                   pl.BlockSpec((v_width, PROJ_ROWS), lambda i: (0, i))],
        out_shape=[jax.ShapeDtypeStruct((n, width), jnp.bfloat16),
                   jax.ShapeDtypeStruct((v_width, n), jnp.bfloat16)],
        compiler_params=pltpu.CompilerParams(vmem_limit_bytes=VMEM_LIMIT),
        name="proj",
    )(x2, g_pre, w_in, w_vt, cos_t, sin_t)


def _neg_softplus(z):
    return -(jnp.maximum(z, 0.0) + jnp.log(1.0 + jnp.exp(-jnp.abs(z))))


def _suffix_sum(lg, tri):
    hi = lg.astype(jnp.bfloat16)
    lo = (lg - hi.astype(jnp.float32)).astype(jnp.bfloat16)
    return (jnp.dot(hi, tri, preferred_element_type=jnp.float32)
            + jnp.dot(lo, tri, preferred_element_type=jnp.float32))


def _sb_kernel(q_ref, k_ref, v_ref, out_ref):
    t = ATT_BLOCK
    qi = pl.program_id(2)
    lane = lax.broadcasted_iota(jnp.int32, (t, LANES), 1)
    head_lanes = [lane < SB_HEAD_DIM, lane >= SB_HEAD_DIM]
    q = q_ref[...]
    qs = [jnp.where(m, q, jnp.zeros_like(q)) for m in head_lanes]
    row = lax.broadcasted_iota(jnp.int32, (t, t), 0)
    col = lax.broadcasted_iota(jnp.int32, (t, t), 1)
    tri = (row >= col).astype(jnp.bfloat16)
    strict = col < row

    def block(kb, carries, masked):
        start = pl.multiple_of(kb * t, t)
        k = k_ref[pl.ds(start, t), :]
        v = v_ref[pl.ds(start, t), :]
        acc = carries[0]
        new_carries = []
        for h in range(2):
            z = lax.dot_general(qs[h], k, _NT, preferred_element_type=jnp.float32)
            lg = _neg_softplus(z)
            if masked:
                lg = jnp.where(strict, lg, 0.0)
            local = _suffix_sum(lg, tri)
            c = local + jnp.concatenate([carries[1 + h]] * (t // LANES), axis=1)
            w = jnp.exp(z + c)
            if masked:
                w = jnp.where(strict, w, 0.0)
            vh = jnp.where(head_lanes[h], v, jnp.zeros_like(v))
            acc = acc + jnp.dot(w.astype(jnp.bfloat16), vh, preferred_element_type=jnp.float32)
            new_carries.append(carries[1 + h] + jnp.broadcast_to(local[:, :1], (t, LANES)))
        return (acc, new_carries[0], new_carries[1])

    def live(carries):
        return (jnp.max(jnp.maximum(carries[1], carries[2])) > SB_DEAD_LOG).astype(jnp.int32)

    def step(loop_state):
        kb, _, carries = loop_state
        carries = block(kb, carries, False)
        return kb - 1, live(carries), carries

    zeros = jnp.zeros((t, LANES), jnp.float32)
    carries = block(qi, (zeros, zeros, zeros), True)
    _, _, carries = lax.while_loop(lambda s: (s[0] >= 0) & (s[1] > 0), step,
                                   (qi - 1, live(carries), carries))
    out_ref[...] = carries[0]


def _sb_attn(proj, batch, seq, q_col, k_col, v_col, n_pairs):
    t = ATT_BLOCK
    nq = seq // t
    return pl.pallas_call(
        _sb_kernel,
        grid=(batch, n_pairs, nq),
        in_specs=[pl.BlockSpec((t, LANES), lambda b, p, i: (b * nq + i, q_col + p)),
                  pl.BlockSpec((seq, LANES), lambda b, p, i: (b, k_col + p)),
                  pl.BlockSpec((seq, LANES), lambda b, p, i: (b, v_col + p))],
        out_specs=pl.BlockSpec((t, LANES), lambda b, p, i: (b * nq + i, p)),
        out_shape=jax.ShapeDtypeStruct((batch * seq, n_pairs * LANES), jnp.float32),
        compiler_params=pltpu.CompilerParams(vmem_limit_bytes=VMEM_LIMIT),
        name="sb_attn",
    )(proj, proj, proj)


def _df_kernel(q_ref, k_ref, vt_ref, lam_ref, g_ref, out_ref, acc_ref, sc0_ref, sc1_ref, *, lambda_init):
    t = ATT_BLOCK
    sc_refs = (sc0_ref, sc1_ref)
    qi = pl.program_id(2)
    lane = lax.broadcasted_iota(jnp.int32, (t, LANES), 1)
    q = q_ref[...]
    qs = [jnp.where(lane // DF_QK_DIM == s, q, jnp.zeros_like(q)) for s in range(4)]
    key = lax.broadcasted_iota(jnp.int32, (t, t), 0)
    qry = lax.broadcasted_iota(jnp.int32, (t, t), 1)
    allowed = (key // CHUNK) <= (qry // CHUNK)
    ones = jnp.ones((DF_ONES_ROWS, t), jnp.bfloat16)

    def score(kb, slot, masked):
        start = pl.multiple_of(kb * t, t)
        k = k_ref[pl.ds(start, t), :]
        m_blk = []
        for s in range(4):
            sc = lax.dot_general(k, qs[s], _NT, preferred_element_type=jnp.float32)
            if masked:
                sc = jnp.where(allowed, sc, -jnp.inf)
            sc_refs[slot][s] = sc
            m_blk.append(jnp.max(sc, axis=0, keepdims=True))
        return tuple(m_blk)

    def value(kb, slot, m_run, m_blk):
        start = pl.multiple_of(kb * t, t)
        vt = vt_ref[:, pl.ds(start, t)]
        vaug = [jnp.concatenate([vt[h * DF_V_DIM:(h + 1) * DF_V_DIM], ones], axis=0) for h in range(2)]
        m_new = tuple(jnp.maximum(m_run[s], m_blk[s]) for s in range(4))
        ps = [jnp.exp2(sc_refs[slot][s] - m_new[s]).astype(jnp.bfloat16) for s in range(4)]
        for s in range(4):
            pv = jnp.dot(vaug[s // 2], ps[s], preferred_element_type=jnp.float32)
            acc_ref[s] = acc_ref[s] * jnp.exp2(m_run[s] - m_new[s]) + pv
        return m_new

    def pair(p, carry):
        m_run, m_blk0 = carry
        kb = qi - 2 * p
        m_blk1 = score(kb - 1, 1, False)
        m_run = value(kb, 0, m_run, m_blk0)
        m_blk0 = score(jnp.maximum(kb - 2, 0), 0, False)
        m_run = value(kb - 1, 1, m_run, m_blk1)
        return m_run, m_blk0

    acc_ref[...] = jnp.zeros(acc_ref.shape, jnp.float32)
    m_init = tuple(jnp.full((1, t), -jnp.inf, jnp.float32) for _ in range(4))
    n_blocks = qi + 1
    m_run, m_blk0 = lax.fori_loop(0, n_blocks // 2, pair, (m_init, score(qi, 0, True)))

    @pl.when(n_blocks % 2 == 1)
    def _():
        value(0, 0, m_run, m_blk0)

    lp = lam_ref[...]
    lam = (jnp.exp(jnp.sum(lp[0:1] * lp[1:2], axis=1, keepdims=True))
           - jnp.exp(jnp.sum(lp[2:3] * lp[3:4], axis=1, keepdims=True)) + lambda_init)
    heads = []
    for h in range(2):
        o = []
        for m in range(2):
            acc = acc_ref[2 * h + m]
            o.append(acc[:DF_V_DIM] / acc[DF_V_DIM:DF_V_DIM + 1])
        oh = o[0] - lam * o[1]
        ms = jnp.mean(oh * oh, axis=0, keepdims=True)
        heads.append(oh * lax.rsqrt(ms + EPS))
    y = jnp.concatenate(heads, axis=0).T
    out_ref[...] = y * g_ref[...] * (1.0 - lambda_init)


def _df_attn(proj, v_t, df_lambda, g_sub2, batch, seq, q_col, k_col, n_pairs, lambda_init):
    t = ATT_BLOCK
    nq = seq // t
    return pl.pallas_call(
        functools.partial(_df_kernel, lambda_init=lambda_init),
        grid=(batch, n_pairs, nq),
        in_specs=[pl.BlockSpec((t, LANES), lambda b, p, i: (b * nq + i, q_col + p)),
                  pl.BlockSpec((seq, LANES), lambda b, p, i: (b, k_col + p)),
                  pl.BlockSpec((LANES, seq), lambda b, p, i: (p, b)),
                  pl.BlockSpec(df_lambda.shape, lambda b, p, i: (0, 0)),
                  pl.BlockSpec((1, LANES), lambda b, p, i: (0, 0))],
        out_specs=pl.BlockSpec((t, LANES), lambda b, p, i: (b * nq + i, p)),
        out_shape=jax.ShapeDtypeStruct((batch * seq, n_pairs * LANES), jnp.float32),
        scratch_shapes=[pltpu.VMEM((4, DF_V_DIM + DF_ONES_ROWS, t), jnp.float32),
                        pltpu.VMEM((4, t, t), jnp.float32),
                        pltpu.VMEM((4, t, t), jnp.float32)],
        compiler_params=pltpu.CompilerParams(vmem_limit_bytes=VMEM_LIMIT),
        name="df_attn",
    )(proj, proj, v_t, df_lambda, g_sub2)


def _silu(g):
    return g * (1.0 / (1.0 + jnp.exp(-g)))


def _merge_kernel(x_ref, ysb_ref, ydf_ref, gsb_ref, gdf_ref, mq_ref, gm_ref, mkv_ref,
                  w_ref, g_ref, out_ref):
    mq = mq_ref[...]
    width = mq.shape[1]
    kv = mkv_ref[0]
    km = kv[:, :width]
    vm = kv[:, width:]
    lane_q = lax.broadcasted_iota(jnp.int32, mq.shape, 1) // MEM_HEAD_DIM
    lane_v = lax.broadcasted_iota(jnp.int32, vm.shape, 1) // MEM_HEAD_DIM
    y_m = jnp.zeros(mq.shape, jnp.float32)
    for h in range(MEM_HEADS):
        qh = jnp.where(lane_q == h, mq, jnp.zeros_like(mq))
        sc = lax.dot_general(qh, km, _NT, preferred_element_type=jnp.float32)
        p = jnp.exp(sc - jnp.max(sc, axis=1, keepdims=True))
        p = p / jnp.sum(p, axis=1, keepdims=True)
        vh = jnp.where(lane_v == h, vm, jnp.zeros_like(vm))
        y_m = y_m + jnp.dot(p.astype(jnp.bfloat16), vh, preferred_element_type=jnp.float32)
    y = jnp.concatenate([ysb_ref[...] * _silu(gsb_ref[...].astype(jnp.float32)),
                         ydf_ref[...] * _silu(gdf_ref[...].astype(jnp.float32)),
                         y_m * _silu(gm_ref[...].astype(jnp.float32))], axis=1)
    o = jnp.dot(y.astype(jnp.bfloat16), w_ref[...], preferred_element_type=jnp.float32)
    out_ref[...] = x_ref[...] + _rms(o, g_ref[...])


def _merge(x2, y_sb, y_df, proj, mkv, w_out, g_post, seq, cols):
    n, d = x2.shape
    r = MERGE_ROWS
    sb_w, df_w, m_w = y_sb.shape[1], y_df.shape[1], mkv.shape[2] // 2
    gsb_col, gdf_col, mq_col, gm_col = cols
    per_batch = seq // r
    return pl.pallas_call(
        _merge_kernel,
        grid=(n // r,),
        in_specs=[pl.BlockSpec((r, d), lambda i: (i, 0)),
                  pl.BlockSpec((r, sb_w), lambda i: (i, 0)),
                  pl.BlockSpec((r, df_w), lambda i: (i, 0)),
                  pl.BlockSpec((r, sb_w), lambda i: (i, gsb_col // sb_w)),
                  pl.BlockSpec((r, df_w), lambda i: (i, gdf_col // df_w)),
                  pl.BlockSpec((r, m_w), lambda i: (i, mq_col // m_w)),
                  pl.BlockSpec((r, m_w), lambda i: (i, gm_col // m_w)),
                  pl.BlockSpec((1,) + mkv.shape[1:], lambda i: (i // per_batch, 0, 0)),
                  pl.BlockSpec(w_out.shape, lambda i: (0, 0)),
                  pl.BlockSpec((1, d), lambda i: (0, 0))],
        out_specs=pl.BlockSpec((r, d), lambda i: (i, 0)),
        out_shape=jax.ShapeDtypeStruct((n, d), jnp.float32),
        compiler_params=pltpu.CompilerParams(vmem_limit_bytes=VMEM_LIMIT),
        name="merge",
    )(x2, y_sb, y_df, proj, proj, proj, proj, mkv, w_out, g_post)


def _rope_tables(positions):
    half = DF_ROT_DIMS // 2
    inv_freq = 1.0 / (ROPE_THETA ** (jnp.arange(0, DF_ROT_DIMS, 2, dtype=jnp.float32) / DF_ROT_DIMS))
    ang = positions.astype(jnp.float32)[:, :, None] * inv_freq
    cos, sin = jnp.cos(ang), jnp.sin(ang)
    pad = DF_QK_DIM - DF_ROT_DIMS
    cos_g = jnp.concatenate([cos, cos, jnp.ones(cos.shape[:2] + (pad,), jnp.float32)], axis=-1)
    sin_g = jnp.concatenate([-sin, sin, jnp.zeros(sin.shape[:2] + (pad,), jnp.float32)], axis=-1)
    reps = LANES // DF_QK_DIM
    b, s = positions.shape
    return (jnp.tile(cos_g, (1, 1, reps)).reshape(b * s, LANES),
            jnp.tile(sin_g, (1, 1, reps)).reshape(b * s, LANES))


def kernel(x, mem, positions, w_in, w_mem_kv, w_out, g_pre, g_post, g_mem, g_subln, df_lambda):
    batch, seq, d = x.shape
    depth = w_in.shape[0]
    sb_w, df_w, m_w = d // 2, d // 4, d // 4
    sb_q, sb_k, sb_v, sb_g = 0, sb_w, 2 * sb_w, 3 * sb_w
    df_q = 4 * sb_w
    df_k, df_v, df_g = df_q + df_w, df_q + 2 * df_w, df_q + 3 * df_w
    m_q, m_g = df_q + 4 * df_w, df_q + 4 * df_w + m_w
    assert m_g + m_w == w_in.shape[2]
    assert seq % ATT_BLOCK == 0 and (batch * seq) % PROJ_ROWS == 0 and seq % MERGE_ROWS == 0

    def layout(col):
        if sb_q <= col < sb_k:
            return SB_HEAD_DIM ** -0.5, False
        if df_q <= col < df_k:
            return DF_QK_DIM ** -0.5 * LOG2E, True
        if df_k <= col < df_v:
            return 1.0, True
        if m_q <= col < m_g:
            return MEM_HEAD_DIM ** -0.5, False
        return 1.0, False

    cos_t, sin_t = _rope_tables(positions)
    x2 = x.reshape(batch * seq, d)
    for layer in range(depth):
        lambda_init = 0.8 - 0.6 * math.exp(-0.3 * layer)
        mkv = _mem_kv(mem, g_mem[layer][None], w_mem_kv[layer].astype(jnp.bfloat16))
        w_in_l = w_in[layer].astype(jnp.bfloat16)
        proj, v_t = _proj(x2, g_pre[layer][None], w_in_l, w_in_l[:, df_v:df_g].T, cos_t, sin_t, layout)
        y_sb = _sb_attn(proj, batch, seq, sb_q // LANES, sb_k // LANES, sb_v // LANES, sb_w // LANES)
        g_sub2 = jnp.tile(g_subln[layer], LANES // DF_V_DIM)[None]
        y_df = _df_attn(proj, v_t, df_lambda[layer], g_sub2, batch, seq,
                        df_q // LANES, df_k // LANES, df_w // LANES, lambda_init)
        x2 = _merge(x2, y_sb, y_df, proj, mkv, w_out[layer].astype(jnp.bfloat16), g_post[layer][None],
                    seq, (sb_g, df_g, m_q, m_g))
    return x2.reshape(batch, seq, d)
```

```python
import functools
import math

import jax
import jax.numpy as jnp
from jax import lax
from jax.experimental import pallas as pl
from jax.experimental.pallas import tpu as pltpu

EPS = 1e-6
ROPE_THETA = 500000.0
CHUNK = 64

LANES = 128
SB_HEAD_DIM = 64
DF_QK_DIM = 32
DF_V_DIM = 64
DF_ROT_DIMS = 8
MEM_HEAD_DIM = 64
MEM_HEADS = 4
DF_ONES_ROWS = 16
DF_UNROLL = 4
LOG2E = 1.4426950408889634

ATT_BLOCK = 256
PROJ_ROWS = 512
MERGE_ROWS = 512
MERGE_SUB_ROWS = 256
PROJ_COL_CHUNK = 512
VMEM_LIMIT = 48 * 1024 * 1024
SB_DEAD_LOG = -110.0

_NT = (((1,), (1,)), ((), ()))


def _rms(xf, g):
    return xf * lax.rsqrt(jnp.mean(xf * xf, axis=-1, keepdims=True) + EPS) * g


def _mem_kv_kernel(mem_ref, g_ref, w_ref, out_ref):
    h = _rms(mem_ref[0], g_ref[...]).astype(jnp.bfloat16)
    out_ref[0] = jnp.dot(h, w_ref[...], preferred_element_type=jnp.float32).astype(out_ref.dtype)


def _mem_kv(mem, g_mem, w_mem_kv):
    b, n_mem, d = mem.shape
    width = w_mem_kv.shape[1]
    return pl.pallas_call(
        _mem_kv_kernel,
        grid=(b,),
        in_specs=[pl.BlockSpec((1, n_mem, d), lambda i: (i, 0, 0)),
                  pl.BlockSpec((1, d), lambda i: (0, 0)),
                  pl.BlockSpec((d, width), lambda i: (0, 0))],
        out_specs=pl.BlockSpec((1, n_mem, width), lambda i: (i, 0, 0)),
        out_shape=jax.ShapeDtypeStruct((b, n_mem, width), jnp.bfloat16),
        name="mem_kv",
    )(mem, g_mem, w_mem_kv)


def _rope_block(p, cos_t, sin_t):
    lane = lax.broadcasted_iota(jnp.int32, p.shape, 1) % DF_QK_DIM
    half = DF_ROT_DIMS // 2
    partner = jnp.where(lane < half, pltpu.roll(p, LANES - half, 1), pltpu.roll(p, half, 1))
    return p * cos_t + partner * sin_t


def _proj_kernel(x_ref, g_ref, w_ref, wvt_ref, cos_ref, sin_ref, out_ref, vt_ref, *, layout):
    h = _rms(x_ref[...], g_ref[...]).astype(jnp.bfloat16)
    n_chunks = w_ref.shape[1] // PROJ_COL_CHUNK
    for c in range(n_chunks):
        c0 = c * PROJ_COL_CHUNK
        p = jnp.dot(h, w_ref[:, c0:c0 + PROJ_COL_CHUNK], preferred_element_type=jnp.float32)
        for j in range(PROJ_COL_CHUNK // LANES):
            col = c0 + j * LANES
            blk = p[:, j * LANES:(j + 1) * LANES]
            scale, rope = layout(col)
            if rope:
                blk = _rope_block(blk, cos_ref[...], sin_ref[...])
            if scale != 1.0:
                blk = blk * scale
            out_ref[:, col:col + LANES] = blk.astype(out_ref.dtype)
    vt_ref[...] = lax.dot_general(wvt_ref[...], h, _NT,
                                  preferred_element_type=jnp.float32).astype(vt_ref.dtype)


def _proj(x2, g_pre, w_in, w_vt, cos_t, sin_t, layout):
    n, d = x2.shape
    width = w_in.shape[1]
    v_width = w_vt.shape[0]
    return pl.pallas_call(
        functools.partial(_proj_kernel, layout=layout),
        grid=(n // PROJ_ROWS,),
        in_specs=[pl.BlockSpec((PROJ_ROWS, d), lambda i: (i, 0)),
                  pl.BlockSpec((1, d), lambda i: (0, 0)),
                  pl.BlockSpec((d, width), lambda i: (0, 0)),
                  pl.BlockSpec((v_width, d), lambda i: (0, 0)),
                  pl.BlockSpec((PROJ_ROWS, LANES), lambda i: (i, 0)),
                  pl.BlockSpec((PROJ_ROWS, LANES), lambda i: (i, 0))],
        out_specs=[pl.BlockSpec((PROJ_ROWS, width), lambda i: (i, 0)),
                   pl.BlockSpec((v_width, PROJ_ROWS), lambda i: (0, i))],
        out_shape=[jax.ShapeDtypeStruct((n, width), jnp.bfloat16),
                   jax.ShapeDtypeStruct((v_width, n), jnp.bfloat16)],
        compiler_params=pltpu.CompilerParams(vmem_limit_bytes=VMEM_LIMIT),
        name="proj",
    )(x2, g_pre, w_in, w_vt, cos_t, sin_t)


def _log_one_minus_beta(nz):
    return jnp.minimum(nz, 0.0) - jnp.log(1.0 + jnp.exp2(jnp.abs(nz) * (-LOG2E)))


def _sb_kernel(q_ref, k_ref, v_ref, out_ref):
    t = ATT_BLOCK
    qi = pl.program_id(2)
    lane = lax.broadcasted_iota(jnp.int32, (t, LANES), 1)
    head_lanes = [lane < SB_HEAD_DIM, lane >= SB_HEAD_DIM]
    q = q_ref[...]
    qs = [jnp.where(m, q, jnp.zeros_like(q)) for m in head_lanes]
    row = lax.broadcasted_iota(jnp.int32, (t, t), 0)
    col = lax.broadcasted_iota(jnp.int32, (t, t), 1)
    tri = (row >= col).astype(jnp.bfloat16)
    tri2 = jnp.concatenate([tri, tri], axis=0)
    strict = col < row

    def run(blocks, carry):
        ks = [k_ref[pl.ds(pl.multiple_of(kb * t, t), t), :] for kb, _ in blocks]
        vs = [v_ref[pl.ds(pl.multiple_of(kb * t, t), t), :] for kb, _ in blocks]
        nz = [[lax.dot_general(qs[h], k, _NT, preferred_element_type=jnp.float32) for h in range(2)]
              for k in ks]
        local = []
        for (_, masked), nz_b in zip(blocks, nz):
            parts = []
            for h in range(2):
                lg = _log_one_minus_beta(nz_b[h])
                if masked:
                    lg = jnp.where(strict, lg, 0.0)
                hi = lg.astype(jnp.bfloat16)
                lo = (lg - hi.astype(jnp.float32)).astype(jnp.bfloat16)
                parts.append(jnp.concatenate([hi, lo], axis=1))
            local.append([jnp.dot(hl, tri2, preferred_element_type=jnp.float32) for hl in parts])
        carry = list(carry)
        ws = []
        for (_, masked), nz_b, local_b in zip(blocks, nz, local):
            w_b = []
            for h in range(2):
                c = local_b[h]
                total = jnp.broadcast_to(c[:, :1], (t, LANES))
                if carry[h] is not None:
                    c = c + jnp.concatenate([carry[h]] * (t // LANES), axis=1)
                    total = total + carry[h]
                w = jnp.exp(c - nz_b[h])
                if masked:
                    w = jnp.where(strict, w, 0.0)
                w_b.append(w.astype(jnp.bfloat16))
                carry[h] = total
            ws.append(w_b)
        pvs = []
        for w_b, v in zip(ws, vs):
            pvs.append(sum(jnp.dot(w_b[h], jnp.where(head_lanes[h], v, jnp.zeros_like(v)),
                                   preferred_element_type=jnp.float32) for h in range(2)))
        return pvs, carry

    def live(carry):
        return (jnp.max(jnp.maximum(carry[0], carry[1])) > SB_DEAD_LOG).astype(jnp.int32)

    def step(loop_state):
        kb, _, acc, c0, c1 = loop_state
        (pv,), carry = run([(kb, False)], (c0, c1))
        return kb - 1, live(carry), acc + pv, carry[0], carry[1]

    (pv_diag, pv_prev), carry = run([(qi, True), (jnp.maximum(qi - 1, 0), False)], (None, None))
    acc = pv_diag + jnp.where(qi > 0, pv_prev, 0.0)
    state = lax.while_loop(lambda s: (s[0] >= 0) & (s[1] > 0), step,
                           (qi - 2, live(carry), acc, carry[0], carry[1]))
    out_ref[...] = state[2]


def _sb_attn(proj, batch, seq, q_col, k_col, v_col, n_pairs):
    t = ATT_BLOCK
    nq = seq // t
    return pl.pallas_call(
        _sb_kernel,
        grid=(batch, n_pairs, nq),
        in_specs=[pl.BlockSpec((t, LANES), lambda b, p, i: (b * nq + i, q_col + p)),
                  pl.BlockSpec((seq, LANES), lambda b, p, i: (b, k_col + p)),
                  pl.BlockSpec((seq, LANES), lambda b, p, i: (b, v_col + p))],
        out_specs=pl.BlockSpec((t, LANES), lambda b, p, i: (b * nq + i, p)),
        out_shape=jax.ShapeDtypeStruct((batch * seq, n_pairs * LANES), jnp.float32),
        compiler_params=pltpu.CompilerParams(vmem_limit_bytes=VMEM_LIMIT),
        name="sb_attn",
    )(proj, proj, proj)


def _df_kernel(q_ref, k_ref, vt_ref, lam_ref, g_ref, out_ref, acc_ref, sc0_ref, sc1_ref, mrun_ref, mblk_ref,
               *, lambda_init):
    t = ATT_BLOCK
    sc_refs = (sc0_ref, sc1_ref)
    qi = pl.program_id(2)
    lane = lax.broadcasted_iota(jnp.int32, (t, LANES), 1)
    q = q_ref[...]
    qs = [jnp.where(lane // DF_QK_DIM == s, q, jnp.zeros_like(q)) for s in range(4)]
    key = lax.broadcasted_iota(jnp.int32, (t, t), 0)
    qry = lax.broadcasted_iota(jnp.int32, (t, t), 1)
    allowed = (key // CHUNK) <= (qry // CHUNK)
    ones = jnp.ones((DF_ONES_ROWS, t), jnp.bfloat16)

    def score(kb, slot, masked):
        start = pl.multiple_of(kb * t, t)
        k = k_ref[pl.ds(start, t), :]
        for s in range(4):
            sc = lax.dot_general(k, qs[s], _NT, preferred_element_type=jnp.float32)
            if masked:
                sc = jnp.where(allowed, sc, -jnp.inf)
            sc_refs[slot][s] = sc
            mblk_ref[slot, s] = jnp.max(sc, axis=0, keepdims=True)

    def value(kb, slot):
        start = pl.multiple_of(kb * t, t)
        vt = vt_ref[:, pl.ds(start, t)]
        vaug = [jnp.concatenate([vt[h * DF_V_DIM:(h + 1) * DF_V_DIM], ones], axis=0) for h in range(2)]
        m_run = [mrun_ref[s] for s in range(4)]
        m_new = [jnp.maximum(m_run[s], mblk_ref[slot, s]) for s in range(4)]
        ps = [jnp.exp2(sc_refs[slot][s] - m_new[s]).astype(jnp.bfloat16) for s in range(4)]
        for s in range(4):
            pv = jnp.dot(vaug[s // 2], ps[s], preferred_element_type=jnp.float32)
            acc_ref[s] = acc_ref[s] * jnp.exp2(m_run[s] - m_new[s]) + pv
            mrun_ref[s] = m_new[s]

    def chain(kb, n):
        for i in range(n):
            score(jnp.maximum(kb - i - 1, 0), (i + 1) % 2, False)
            value(kb - i, i % 2)

    acc_ref[...] = jnp.zeros(acc_ref.shape, jnp.float32)
    mrun_ref[...] = jnp.full(mrun_ref.shape, -jnp.inf, jnp.float32)
    score(qi, 0, True)
    n_blocks = qi + 1
    n_trips = n_blocks // DF_UNROLL

    def trip(p, carry):
        chain(qi - DF_UNROLL * p, DF_UNROLL)
        return carry

    lax.fori_loop(0, n_trips, trip, 0)
    left = n_blocks - DF_UNROLL * n_trips

    @pl.when(left >= 2)
    def _():
        chain(left - 1, 2)

    @pl.when(left % 2 == 1)
    def _():
        value(0, 0)

    lp = lam_ref[...]
    lam = (jnp.exp(jnp.sum(lp[0:1] * lp[1:2], axis=1, keepdims=True))
           - jnp.exp(jnp.sum(lp[2:3] * lp[3:4], axis=1, keepdims=True)) + lambda_init)
    heads = []
    for h in range(2):
        o = []
        for m in range(2):
            acc = acc_ref[2 * h + m]
            o.append(acc[:DF_V_DIM] / acc[DF_V_DIM:DF_V_DIM + 1])
        oh = o[0] - lam * o[1]
        ms = jnp.mean(oh * oh, axis=0, keepdims=True)
        heads.append(oh * lax.rsqrt(ms + EPS))
    y = jnp.concatenate(heads, axis=0).T
    out_ref[...] = y * g_ref[...] * (1.0 - lambda_init)


def _df_attn(proj, v_t, df_lambda, g_sub2, batch, seq, q_col, k_col, n_pairs, lambda_init):
    t = ATT_BLOCK
    nq = seq // t
    return pl.pallas_call(
        functools.partial(_df_kernel, lambda_init=lambda_init),
        grid=(batch, n_pairs, nq),
        in_specs=[pl.BlockSpec((t, LANES), lambda b, p, i: (b * nq + i, q_col + p)),
                  pl.BlockSpec((seq, LANES), lambda b, p, i: (b, k_col + p)),
                  pl.BlockSpec((LANES, seq), lambda b, p, i: (p, b)),
                  pl.BlockSpec(df_lambda.shape, lambda b, p, i: (0, 0)),
                  pl.BlockSpec((1, LANES), lambda b, p, i: (0, 0))],
        out_specs=pl.BlockSpec((t, LANES), lambda b, p, i: (b * nq + i, p)),
        out_shape=jax.ShapeDtypeStruct((batch * seq, n_pairs * LANES), jnp.float32),
        scratch_shapes=[pltpu.VMEM((4, DF_V_DIM + DF_ONES_ROWS, t), jnp.float32),
                        pltpu.VMEM((4, t, t), jnp.float32),
                        pltpu.VMEM((4, t, t), jnp.float32),
                        pltpu.VMEM((4, 1, t), jnp.float32),
                        pltpu.VMEM((2, 4, 1, t), jnp.float32)],
        compiler_params=pltpu.CompilerParams(vmem_limit_bytes=VMEM_LIMIT),
        name="df_attn",
    )(proj, proj, v_t, df_lambda, g_sub2)


def _silu(g):
    return g * (1.0 / (1.0 + jnp.exp(-g)))


def _merge_kernel(x_ref, ysb_ref, ydf_ref, gsb_ref, gdf_ref, mq_ref, gm_ref, mkv_ref,
                  w_ref, g_ref, out_ref):
    width = mq_ref.shape[1]
    kv = mkv_ref[0]
    km = kv[:, :width]
    vm = kv[:, width:]
    sub = MERGE_SUB_ROWS
    lane_q = lax.broadcasted_iota(jnp.int32, (sub, width), 1) // MEM_HEAD_DIM
    lane_v = lax.broadcasted_iota(jnp.int32, vm.shape, 1) // MEM_HEAD_DIM
    vms = [jnp.where(lane_v == h, vm, jnp.zeros_like(vm)) for h in range(MEM_HEADS)]
    rows = [pl.ds(i * sub, sub) for i in range(mq_ref.shape[0] // sub)]
    scs = []
    for r in rows:
        mq = mq_ref[r, :]
        scs.append([lax.dot_general(jnp.where(lane_q == h, mq, jnp.zeros_like(mq)), km, _NT,
                                    preferred_element_type=jnp.float32) for h in range(MEM_HEADS)])
    ps = []
    for sc_r in scs:
        p_r = []
        for sc in sc_r:
            p = jnp.exp2(sc - jnp.max(sc, axis=1, keepdims=True))
            p_r.append((p * (1.0 / jnp.sum(p, axis=1, keepdims=True))).astype(jnp.bfloat16))
        ps.append(p_r)
    y_ms = [sum(jnp.dot(p_r[h], vms[h], preferred_element_type=jnp.float32) for h in range(MEM_HEADS))
            for p_r in ps]
    ys = []
    for r, y_m in zip(rows, y_ms):
        ys.append(jnp.concatenate([ysb_ref[r, :] * _silu(gsb_ref[r, :].astype(jnp.float32)),
                                   ydf_ref[r, :] * _silu(gdf_ref[r, :].astype(jnp.float32)),
                                   y_m * _silu(gm_ref[r, :].astype(jnp.float32))], axis=1).astype(jnp.bfloat16))
    os_ = [jnp.dot(y, w_ref[...], preferred_element_type=jnp.float32) for y in ys]
    for r, o in zip(rows, os_):
        out_ref[r, :] = x_ref[r, :] + _rms(o, g_ref[...])


def _merge(x2, y_sb, y_df, proj, mkv, w_out, g_post, seq, cols):
    n, d = x2.shape
    r = MERGE_ROWS
    sb_w, df_w, m_w = y_sb.shape[1], y_df.shape[1], mkv.shape[2] // 2
    gsb_col, gdf_col, mq_col, gm_col = cols
    per_batch = seq // r
    return pl.pallas_call(
        _merge_kernel,
        grid=(n // r,),
        in_specs=[pl.BlockSpec((r, d), lambda i: (i, 0)),
                  pl.BlockSpec((r, sb_w), lambda i: (i, 0)),
                  pl.BlockSpec((r, df_w), lambda i: (i, 0)),
                  pl.BlockSpec((r, sb_w), lambda i: (i, gsb_col // sb_w)),
                  pl.BlockSpec((r, df_w), lambda i: (i, gdf_col // df_w)),
                  pl.BlockSpec((r, m_w), lambda i: (i, mq_col // m_w)),
                  pl.BlockSpec((r, m_w), lambda i: (i, gm_col // m_w)),
                  pl.BlockSpec((1,) + mkv.shape[1:], lambda i: (i // per_batch, 0, 0)),
                  pl.BlockSpec(w_out.shape, lambda i: (0, 0)),
                  pl.BlockSpec((1, d), lambda i: (0, 0))],
        out_specs=pl.BlockSpec((r, d), lambda i: (i, 0)),
        out_shape=jax.ShapeDtypeStruct((n, d), jnp.float32),
        compiler_params=pltpu.CompilerParams(vmem_limit_bytes=VMEM_LIMIT),
        name="merge",
    )(x2, y_sb, y_df, proj, proj, proj, proj, mkv, w_out, g_post)


def _rope_tables(positions):
    half = DF_ROT_DIMS // 2
    inv_freq = 1.0 / (ROPE_THETA ** (jnp.arange(0, DF_ROT_DIMS, 2, dtype=jnp.float32) / DF_ROT_DIMS))
    ang = positions.astype(jnp.float32)[:, :, None] * inv_freq
    cos, sin = jnp.cos(ang), jnp.sin(ang)
    pad = DF_QK_DIM - DF_ROT_DIMS
    cos_g = jnp.concatenate([cos, cos, jnp.ones(cos.shape[:2] + (pad,), jnp.float32)], axis=-1)
    sin_g = jnp.concatenate([-sin, sin, jnp.zeros(sin.shape[:2] + (pad,), jnp.float32)], axis=-1)
    reps = LANES // DF_QK_DIM
    b, s = positions.shape
    return (jnp.tile(cos_g, (1, 1, reps)).reshape(b * s, LANES),
            jnp.tile(sin_g, (1, 1, reps)).reshape(b * s, LANES))


def kernel(x, mem, positions, w_in, w_mem_kv, w_out, g_pre, g_post, g_mem, g_subln, df_lambda):
    batch, seq, d = x.shape
    depth = w_in.shape[0]
    sb_w, df_w, m_w = d // 2, d // 4, d // 4
    sb_q, sb_k, sb_v, sb_g = 0, sb_w, 2 * sb_w, 3 * sb_w
    df_q = 4 * sb_w
    df_k, df_v, df_g = df_q + df_w, df_q + 2 * df_w, df_q + 3 * df_w
    m_q, m_g = df_q + 4 * df_w, df_q + 4 * df_w + m_w
    assert m_g + m_w == w_in.shape[2]
    assert seq % ATT_BLOCK == 0 and (batch * seq) % PROJ_ROWS == 0 and seq % MERGE_ROWS == 0

    def layout(col):
        if sb_q <= col < sb_k:
            return -(SB_HEAD_DIM ** -0.5), False
        if df_q <= col < df_k:
            return DF_QK_DIM ** -0.5 * LOG2E, True
        if df_k <= col < df_v:
            return 1.0, True
        if m_q <= col < m_g:
            return MEM_HEAD_DIM ** -0.5 * LOG2E, False
        return 1.0, False

    cos_t, sin_t = _rope_tables(positions)
    x2 = x.reshape(batch * seq, d)
    for layer in range(depth):
        lambda_init = 0.8 - 0.6 * math.exp(-0.3 * layer)
        mkv = _mem_kv(mem, g_mem[layer][None], w_mem_kv[layer].astype(jnp.bfloat16))
        w_vt = w_in[layer][:, df_v:df_g].T.astype(jnp.bfloat16)
        proj, v_t = _proj(x2, g_pre[layer][None], w_in[layer].astype(jnp.bfloat16), w_vt, cos_t, sin_t, layout)
        y_sb = _sb_attn(proj, batch, seq, sb_q // LANES, sb_k // LANES, sb_v // LANES, sb_w // LANES)
        g_sub2 = jnp.tile(g_subln[layer], LANES // DF_V_DIM)[None]
        y_df = _df_attn(proj, v_t, df_lambda[layer], g_sub2, batch, seq,
                        df_q // LANES, df_k // LANES, df_w // LANES, lambda_init)
        x2 = _merge(x2, y_sb, y_df, proj, mkv, w_out[layer].astype(jnp.bfloat16), g_post[layer][None],
                    seq, (sb_g, df_g, m_q, m_g))
    return x2.reshape(batch, seq, d)
```

```python
import functools
import math

import jax
import jax.numpy as jnp
from jax import lax
from jax.experimental import pallas as pl
from jax.experimental.pallas import tpu as pltpu

EPS = 1e-6
ROPE_THETA = 500000.0
CHUNK = 64

LANES = 128
SB_HEAD_DIM = 64
DF_QK_DIM = 32
DF_V_DIM = 64
DF_ROT_DIMS = 8
MEM_HEAD_DIM = 64
MEM_HEADS = 4
DF_ONES_ROWS = 16
DF_UNROLL = 4
LOG2E = 1.4426950408889634

ATT_BLOCK = 256
SB_Q_BLOCKS = 2
PROJ_ROWS = 512
MERGE_ROWS = 512
MERGE_SUB_ROWS = 256
PROJ_COL_CHUNK = 512
VMEM_LIMIT = 48 * 1024 * 1024
SB_DEAD_LOG = -110.0

_NT = (((1,), (1,)), ((), ()))


def _rms(xf, g):
    return xf * lax.rsqrt(jnp.mean(xf * xf, axis=-1, keepdims=True) + EPS) * g


def _mem_kv_kernel(mem_ref, g_ref, w_ref, out_ref):
    h = _rms(mem_ref[0], g_ref[...]).astype(jnp.bfloat16)
    out_ref[0] = jnp.dot(h, w_ref[...], preferred_element_type=jnp.float32).astype(out_ref.dtype)


def _mem_kv(mem, g_mem, w_mem_kv):
    b, n_mem, d = mem.shape
    width = w_mem_kv.shape[1]
    return pl.pallas_call(
        _mem_kv_kernel,
        grid=(b,),
        in_specs=[pl.BlockSpec((1, n_mem, d), lambda i: (i, 0, 0)),
                  pl.BlockSpec((1, d), lambda i: (0, 0)),
                  pl.BlockSpec((d, width), lambda i: (0, 0))],
        out_specs=pl.BlockSpec((1, n_mem, width), lambda i: (i, 0, 0)),
        out_shape=jax.ShapeDtypeStruct((b, n_mem, width), jnp.bfloat16),
        name="mem_kv",
    )(mem, g_mem, w_mem_kv)


def _rope_block(p, cos_t, sin_t):
    lane = lax.broadcasted_iota(jnp.int32, p.shape, 1) % DF_QK_DIM
    half = DF_ROT_DIMS // 2
    partner = jnp.where(lane < half, pltpu.roll(p, LANES - half, 1), pltpu.roll(p, half, 1))
    return p * cos_t + partner * sin_t


def _proj_kernel(x_ref, g_ref, w_ref, cos_ref, sin_ref, out_ref, vt_ref, *, layout, vt_cols):
    h = _rms(x_ref[...], g_ref[...]).astype(jnp.bfloat16)
    n_chunks = w_ref.shape[1] // PROJ_COL_CHUNK
    for c in range(n_chunks):
        c0 = c * PROJ_COL_CHUNK
        p = jnp.dot(h, w_ref[:, c0:c0 + PROJ_COL_CHUNK], preferred_element_type=jnp.float32)
        for j in range(PROJ_COL_CHUNK // LANES):
            col = c0 + j * LANES
            blk = p[:, j * LANES:(j + 1) * LANES]
            if vt_cols[0] <= col < vt_cols[1]:
                vt_ref[col - vt_cols[0]:col - vt_cols[0] + LANES, :] = blk.T.astype(vt_ref.dtype)
            scale, rope = layout(col)
            if rope:
                blk = _rope_block(blk, cos_ref[...], sin_ref[...])
            if scale != 1.0:
                blk = blk * scale
            out_ref[:, col:col + LANES] = blk.astype(out_ref.dtype)


def _proj(x2, g_pre, w_in, cos_t, sin_t, layout, vt_cols):
    n, d = x2.shape
    width = w_in.shape[1]
    v_width = vt_cols[1] - vt_cols[0]
    return pl.pallas_call(
        functools.partial(_proj_kernel, layout=layout, vt_cols=vt_cols),
        grid=(n // PROJ_ROWS,),
        in_specs=[pl.BlockSpec((PROJ_ROWS, d), lambda i: (i, 0)),
                  pl.BlockSpec((1, d), lambda i: (0, 0)),
                  pl.BlockSpec((d, width), lambda i: (0, 0)),
                  pl.BlockSpec((PROJ_ROWS, LANES), lambda i: (i, 0)),
                  pl.BlockSpec((PROJ_ROWS, LANES), lambda i: (i, 0))],
        out_specs=[pl.BlockSpec((PROJ_ROWS, width), lambda i: (i, 0)),
                   pl.BlockSpec((v_width, PROJ_ROWS), lambda i: (0, i))],
        out_shape=[jax.ShapeDtypeStruct((n, width), jnp.bfloat16),
                   jax.ShapeDtypeStruct((v_width, n), jnp.bfloat16)],
        compiler_params=pltpu.CompilerParams(vmem_limit_bytes=VMEM_LIMIT),
        name="proj",
    )(x2, g_pre, w_in, cos_t, sin_t)


def _log_one_minus_beta(nz):
    return jnp.minimum(nz, 0.0) - jnp.log(1.0 + jnp.exp2(jnp.abs(nz) * (-LOG2E)))


def _sb_kernel(q_ref, k_ref, v_ref, out_ref):
    t = ATT_BLOCK
    first_q = pl.program_id(2) * SB_Q_BLOCKS
    lane = lax.broadcasted_iota(jnp.int32, (t, LANES), 1)
    head_lanes = [lane < SB_HEAD_DIM, lane >= SB_HEAD_DIM]
    qs = []
    for j in range(SB_Q_BLOCKS):
        q = q_ref[j * t:(j + 1) * t, :]
        qs.append([jnp.where(m, q, jnp.zeros_like(q)) for m in head_lanes])
    row = lax.broadcasted_iota(jnp.int32, (t, t), 0)
    col = lax.broadcasted_iota(jnp.int32, (t, t), 1)
    tri = (row >= col).astype(jnp.bfloat16)
    tri2 = jnp.concatenate([tri, tri], axis=0)
    strict = col < row

    def run(blocks, carry):
        ks = [k_ref[pl.ds(pl.multiple_of(kb * t, t), t), :] for _, kb, _ in blocks]
        vs = [v_ref[pl.ds(pl.multiple_of(kb * t, t), t), :] for _, kb, _ in blocks]
        nz = [[lax.dot_general(qs[j][h], k, _NT, preferred_element_type=jnp.float32) for h in range(2)]
              for (j, _, _), k in zip(blocks, ks)]
        local = []
        for (_, _, masked), nz_b in zip(blocks, nz):
            parts = []
            for h in range(2):
                lg = _log_one_minus_beta(nz_b[h])
                if masked:
                    lg = jnp.where(strict, lg, 0.0)
                hi = lg.astype(jnp.bfloat16)
                lo = (lg - hi.astype(jnp.float32)).astype(jnp.bfloat16)
                parts.append(jnp.concatenate([hi, lo], axis=1))
            local.append([jnp.dot(hl, tri2, preferred_element_type=jnp.float32) for hl in parts])
        carry = {j: list(c) for j, c in carry.items()}
        ws = []
        for (j, _, masked), nz_b, local_b in zip(blocks, nz, local):
            w_b = []
            for h in range(2):
                c = local_b[h]
                total = jnp.broadcast_to(c[:, :1], (t, LANES))
                if carry[j][h] is not None:
                    c = c + jnp.concatenate([carry[j][h]] * (t // LANES), axis=1)
                    total = total + carry[j][h]
                w = jnp.exp(c - nz_b[h])
                if masked:
                    w = jnp.where(strict, w, 0.0)
                w_b.append(w.astype(jnp.bfloat16))
                carry[j][h] = total
            ws.append(w_b)
        pvs = []
        for w_b, v in zip(ws, vs):
            pvs.append(sum(jnp.dot(w_b[h], jnp.where(head_lanes[h], v, jnp.zeros_like(v)),
                                   preferred_element_type=jnp.float32) for h in range(2)))
        return pvs, carry

    def live(c):
        return (jnp.max(jnp.maximum(c[0], c[1])) > SB_DEAD_LOG).astype(jnp.int32)

    blocks = []
    for j in range(SB_Q_BLOCKS):
        blocks += [(j, first_q + j, True), (j, jnp.maximum(first_q + j - 1, 0), False)]
    pvs, carry = run(blocks, {j: (None, None) for j in range(SB_Q_BLOCKS)})

    for j in range(SB_Q_BLOCKS):
        pv_prev = pvs[2 * j + 1] if j > 0 else jnp.where(first_q > 0, pvs[1], 0.0)

        def step(loop_state, j=j):
            kb, _, acc, c0, c1 = loop_state
            (pv,), c = run([(j, kb, False)], {j: (c0, c1)})
            return kb - 1, live(c[j]), acc + pv, c[j][0], c[j][1]

        state = lax.while_loop(lambda s: (s[0] >= 0) & (s[1] > 0), step,
                               (first_q + j - 2, live(carry[j]), pvs[2 * j] + pv_prev,
                                carry[j][0], carry[j][1]))
        out_ref[j * t:(j + 1) * t, :] = state[2]


def _sb_attn(proj, batch, seq, q_col, k_col, v_col, n_pairs):
    t = ATT_BLOCK * SB_Q_BLOCKS
    nq = seq // t
    return pl.pallas_call(
        _sb_kernel,
        grid=(batch, n_pairs, nq),
        in_specs=[pl.BlockSpec((t, LANES), lambda b, p, i: (b * nq + i, q_col + p)),
                  pl.BlockSpec((seq, LANES), lambda b, p, i: (b, k_col + p)),
                  pl.BlockSpec((seq, LANES), lambda b, p, i: (b, v_col + p))],
        out_specs=pl.BlockSpec((t, LANES), lambda b, p, i: (b * nq + i, p)),
        out_shape=jax.ShapeDtypeStruct((batch * seq, n_pairs * LANES), jnp.float32),
        compiler_params=pltpu.CompilerParams(vmem_limit_bytes=VMEM_LIMIT),
        name="sb_attn",
    )(proj, proj, proj)


def _df_kernel(q_ref, k_ref, vt_ref, lam_ref, g_ref, out_ref, acc_ref, sc0_ref, sc1_ref, mrun_ref, mblk_ref,
               *, lambda_init):
    t = ATT_BLOCK
    sc_refs = (sc0_ref, sc1_ref)
    qi = pl.program_id(2)
    lane = lax.broadcasted_iota(jnp.int32, (t, LANES), 1)
    q = q_ref[...]
    qs = [jnp.where(lane // DF_QK_DIM == s, q, jnp.zeros_like(q)) for s in range(4)]
    key = lax.broadcasted_iota(jnp.int32, (t, t), 0)
    qry = lax.broadcasted_iota(jnp.int32, (t, t), 1)
    allowed = (key // CHUNK) <= (qry // CHUNK)
    ones = jnp.ones((DF_ONES_ROWS, t), jnp.bfloat16)

    def score(kb, slot, masked):
        start = pl.multiple_of(kb * t, t)
        k = k_ref[pl.ds(start, t), :]
        for s in range(4):
            sc = lax.dot_general(k, qs[s], _NT, preferred_element_type=jnp.float32)
            if masked:
                sc = jnp.where(allowed, sc, -jnp.inf)
            sc_refs[slot][s] = sc
            mblk_ref[slot, s] = jnp.max(sc, axis=0, keepdims=True)

    def value(kb, slot):
        start = pl.multiple_of(kb * t, t)
        vt = vt_ref[:, pl.ds(start, t)]
        vaug = [jnp.concatenate([vt[h * DF_V_DIM:(h + 1) * DF_V_DIM], ones], axis=0) for h in range(2)]
        m_run = [mrun_ref[s] for s in range(4)]
        m_new = [jnp.maximum(m_run[s], mblk_ref[slot, s]) for s in range(4)]
        ps = [jnp.exp2(sc_refs[slot][s] - m_new[s]).astype(jnp.bfloat16) for s in range(4)]
        for s in range(4):
            pv = jnp.dot(vaug[s // 2], ps[s], preferred_element_type=jnp.float32)
            acc_ref[s] = acc_ref[s] * jnp.exp2(m_run[s] - m_new[s]) + pv
            mrun_ref[s] = m_new[s]

    def chain(kb, n):
        for i in range(n):
            score(jnp.maximum(kb - i - 1, 0), (i + 1) % 2, False)
            value(kb - i, i % 2)

    acc_ref[...] = jnp.zeros(acc_ref.shape, jnp.float32)
    mrun_ref[...] = jnp.full(mrun_ref.shape, -jnp.inf, jnp.float32)
    score(qi, 0, True)
    n_blocks = qi + 1
    n_trips = n_blocks // DF_UNROLL

    def trip(p, carry):
        chain(qi - DF_UNROLL * p, DF_UNROLL)
        return carry

    lax.fori_loop(0, n_trips, trip, 0)
    left = n_blocks - DF_UNROLL * n_trips

    @pl.when(left >= 2)
    def _():
        chain(left - 1, 2)

    @pl.when(left % 2 == 1)
    def _():
        value(0, 0)

    lp = lam_ref[...]
    lam = (jnp.exp(jnp.sum(lp[0:1] * lp[1:2], axis=1, keepdims=True))
           - jnp.exp(jnp.sum(lp[2:3] * lp[3:4], axis=1, keepdims=True)) + lambda_init)
    heads = []
    for h in range(2):
        o = []
        for m in range(2):
            acc = acc_ref[2 * h + m]
            o.append(acc[:DF_V_DIM] / acc[DF_V_DIM:DF_V_DIM + 1])
        oh = o[0] - lam * o[1]
        ms = jnp.mean(oh * oh, axis=0, keepdims=True)
        heads.append(oh * lax.rsqrt(ms + EPS))
    y = jnp.concatenate(heads, axis=0).T
    out_ref[...] = y * g_ref[...] * (1.0 - lambda_init)


def _df_attn(proj, v_t, df_lambda, g_sub2, batch, seq, q_col, k_col, n_pairs, lambda_init):
    t = ATT_BLOCK
    nq = seq // t
    return pl.pallas_call(
        functools.partial(_df_kernel, lambda_init=lambda_init),
        grid=(batch, n_pairs, nq),
        in_specs=[pl.BlockSpec((t, LANES), lambda b, p, i: (b * nq + i, q_col + p)),
                  pl.BlockSpec((seq, LANES), lambda b, p, i: (b, k_col + p)),
                  pl.BlockSpec((LANES, seq), lambda b, p, i: (p, b)),
                  pl.BlockSpec(df_lambda.shape, lambda b, p, i: (0, 0)),
                  pl.BlockSpec((1, LANES), lambda b, p, i: (0, 0))],
        out_specs=pl.BlockSpec((t, LANES), lambda b, p, i: (b * nq + i, p)),
        out_shape=jax.ShapeDtypeStruct((batch * seq, n_pairs * LANES), jnp.float32),
        scratch_shapes=[pltpu.VMEM((4, DF_V_DIM + DF_ONES_ROWS, t), jnp.float32),
                        pltpu.VMEM((4, t, t), jnp.float32),
                        pltpu.VMEM((4, t, t), jnp.float32),
                        pltpu.VMEM((4, 1, t), jnp.float32),
                        pltpu.VMEM((2, 4, 1, t), jnp.float32)],
        compiler_params=pltpu.CompilerParams(vmem_limit_bytes=VMEM_LIMIT),
        name="df_attn",
    )(proj, proj, v_t, df_lambda, g_sub2)


def _silu(g):
    return g * (1.0 / (1.0 + jnp.exp(-g)))


def _merge_kernel(x_ref, ysb_ref, ydf_ref, gsb_ref, gdf_ref, mq_ref, gm_ref, mkv_ref,
                  w_ref, g_ref, out_ref):
    width = mq_ref.shape[1]
    kv = mkv_ref[0]
    km = kv[:, :width]
    vm = kv[:, width:]
    sub = MERGE_SUB_ROWS
    lane_q = lax.broadcasted_iota(jnp.int32, (sub, width), 1) // MEM_HEAD_DIM
    lane_v = lax.broadcasted_iota(jnp.int32, vm.shape, 1) // MEM_HEAD_DIM
    vms = [jnp.where(lane_v == h, vm, jnp.zeros_like(vm)) for h in range(MEM_HEADS)]
    rows = [pl.ds(i * sub, sub) for i in range(mq_ref.shape[0] // sub)]
    scs = []
    for r in rows:
        mq = mq_ref[r, :]
        scs.append([lax.dot_general(jnp.where(lane_q == h, mq, jnp.zeros_like(mq)), km, _NT,
                                    preferred_element_type=jnp.float32) for h in range(MEM_HEADS)])
    ps = []
    for sc_r in scs:
        p_r = []
        for sc in sc_r:
            p = jnp.exp2(sc - jnp.max(sc, axis=1, keepdims=True))
            p_r.append((p * (1.0 / jnp.sum(p, axis=1, keepdims=True))).astype(jnp.bfloat16))
        ps.append(p_r)
    y_ms = [sum(jnp.dot(p_r[h], vms[h], preferred_element_type=jnp.float32) for h in range(MEM_HEADS))
            for p_r in ps]
    ys = []
    for r, y_m in zip(rows, y_ms):
        ys.append(jnp.concatenate([ysb_ref[r, :] * _silu(gsb_ref[r, :].astype(jnp.float32)),
                                   ydf_ref[r, :] * _silu(gdf_ref[r, :].astype(jnp.float32)),
                                   y_m * _silu(gm_ref[r, :].astype(jnp.float32))], axis=1).astype(jnp.bfloat16))
    os_ = [jnp.dot(y, w_ref[...], preferred_element_type=jnp.float32) for y in ys]
    for r, o in zip(rows, os_):
        out_ref[r, :] = x_ref[r, :] + _rms(o, g_ref[...])


def _merge(x2, y_sb, y_df, proj, mkv, w_out, g_post, seq, cols):
    n, d = x2.shape
    r = MERGE_ROWS
    sb_w, df_w, m_w = y_sb.shape[1], y_df.shape[1], mkv.shape[2] // 2
    gsb_col, gdf_col, mq_col, gm_col = cols
    per_batch = seq // r
    return pl.pallas_call(
        _merge_kernel,
        grid=(n // r,),
        in_specs=[pl.BlockSpec((r, d), lambda i: (i, 0)),
                  pl.BlockSpec((r, sb_w), lambda i: (i, 0)),
                  pl.BlockSpec((r, df_w), lambda i: (i, 0)),
                  pl.BlockSpec((r, sb_w), lambda i: (i, gsb_col // sb_w)),
                  pl.BlockSpec((r, df_w), lambda i: (i, gdf_col // df_w)),
                  pl.BlockSpec((r, m_w), lambda i: (i, mq_col // m_w)),
                  pl.BlockSpec((r, m_w), lambda i: (i, gm_col // m_w)),
                  pl.BlockSpec((1,) + mkv.shape[1:], lambda i: (i // per_batch, 0, 0)),
                  pl.BlockSpec(w_out.shape, lambda i: (0, 0)),
                  pl.BlockSpec((1, d), lambda i: (0, 0))],
        out_specs=pl.BlockSpec((r, d), lambda i: (i, 0)),
        out_shape=jax.ShapeDtypeStruct((n, d), jnp.float32),
        compiler_params=pltpu.CompilerParams(vmem_limit_bytes=VMEM_LIMIT),
        name="merge",
    )(x2, y_sb, y_df, proj, proj, proj, proj, mkv, w_out, g_post)


def _rope_tables(positions):
    half = DF_ROT_DIMS // 2
    inv_freq = 1.0 / (ROPE_THETA ** (jnp.arange(0, DF_ROT_DIMS, 2, dtype=jnp.float32) / DF_ROT_DIMS))
    ang = positions.astype(jnp.float32)[:, :, None] * inv_freq
    cos, sin = jnp.cos(ang), jnp.sin(ang)
    pad = DF_QK_DIM - DF_ROT_DIMS
    cos_g = jnp.concatenate([cos, cos, jnp.ones(cos.shape[:2] + (pad,), jnp.float32)], axis=-1)
    sin_g = jnp.concatenate([-sin, sin, jnp.zeros(sin.shape[:2] + (pad,), jnp.float32)], axis=-1)
    reps = LANES // DF_QK_DIM
    b, s = positions.shape
    return (jnp.tile(cos_g, (1, 1, reps)).reshape(b * s, LANES),
            jnp.tile(sin_g, (1, 1, reps)).reshape(b * s, LANES))


def kernel(x, mem, positions, w_in, w_mem_kv, w_out, g_pre, g_post, g_mem, g_subln, df_lambda):
    batch, seq, d = x.shape
    depth = w_in.shape[0]
    sb_w, df_w, m_w = d // 2, d // 4, d // 4
    sb_q, sb_k, sb_v, sb_g = 0, sb_w, 2 * sb_w, 3 * sb_w
    df_q = 4 * sb_w
    df_k, df_v, df_g = df_q + df_w, df_q + 2 * df_w, df_q + 3 * df_w
    m_q, m_g = df_q + 4 * df_w, df_q + 4 * df_w + m_w
    assert m_g + m_w == w_in.shape[2]
    assert seq % (ATT_BLOCK * SB_Q_BLOCKS) == 0 and (batch * seq) % PROJ_ROWS == 0 and seq % MERGE_ROWS == 0

    def layout(col):
        if sb_q <= col < sb_k:
            return -(SB_HEAD_DIM ** -0.5), False
        if df_q <= col < df_k:
            return DF_QK_DIM ** -0.5 * LOG2E, True
        if df_k <= col < df_v:
            return 1.0, True
        if m_q <= col < m_g:
            return MEM_HEAD_DIM ** -0.5 * LOG2E, False
        return 1.0, False

    cos_t, sin_t = _rope_tables(positions)
    x2 = x.reshape(batch * seq, d)
    for layer in range(depth):
        lambda_init = 0.8 - 0.6 * math.exp(-0.3 * layer)
        mkv = _mem_kv(mem, g_mem[layer][None], w_mem_kv[layer].astype(jnp.bfloat16))
        proj, v_t = _proj(x2, g_pre[layer][None], w_in[layer].astype(jnp.bfloat16), cos_t, sin_t, layout,
                          (df_v, df_g))
        y_sb = _sb_attn(proj, batch, seq, sb_q // LANES, sb_k // LANES, sb_v // LANES, sb_w // LANES)
        g_sub2 = jnp.tile(g_subln[layer], LANES // DF_V_DIM)[None]
        y_df = _df_attn(proj, v_t, df_lambda[layer], g_sub2, batch, seq,
                        df_q // LANES, df_k // LANES, df_w // LANES, lambda_init)
        x2 = _merge(x2, y_sb, y_df, proj, mkv, w_out[layer].astype(jnp.bfloat16), g_post[layer][None],
                    seq, (sb_g, df_g, m_q, m_g))
    return x2.reshape(batch, seq, d)
```

```python
import functools
import math

import jax
import jax.numpy as jnp
from jax import lax
from jax.experimental import pallas as pl
from jax.experimental.pallas import tpu as pltpu

EPS = 1e-6
ROPE_THETA = 500000.0
CHUNK = 64

LANES = 128
SB_HEAD_DIM = 64
DF_QK_DIM = 32
DF_V_DIM = 64
DF_ROT_DIMS = 8
MEM_HEAD_DIM = 64
MEM_HEADS = 4
DF_ONES_ROWS = 16
DF_UNROLL = 4
LOG2E = 1.4426950408889634

ATT_BLOCK = 256
SB_Q_BLOCKS = 2
PROJ_ROWS = 512
MERGE_ROWS = 512
MERGE_SUB_ROWS = 256
PROJ_COL_CHUNK = 512
VMEM_LIMIT = 48 * 1024 * 1024
SB_DEAD_LOG = -110.0

_NT = (((1,), (1,)), ((), ()))


def _rms(xf, g):
    return xf * lax.rsqrt(jnp.mean(xf * xf, axis=-1, keepdims=True) + EPS) * g


def _mem_kv_kernel(mem_ref, g_ref, w_ref, out_ref):
    h = _rms(mem_ref[0], g_ref[...]).astype(jnp.bfloat16)
    out_ref[0] = jnp.dot(h, w_ref[...], preferred_element_type=jnp.float32).astype(out_ref.dtype)


def _mem_kv(mem, g_mem, w_mem_kv):
    b, n_mem, d = mem.shape
    width = w_mem_kv.shape[1]
    return pl.pallas_call(
        _mem_kv_kernel,
        grid=(b,),
        in_specs=[pl.BlockSpec((1, n_mem, d), lambda i: (i, 0, 0)),
                  pl.BlockSpec((1, d), lambda i: (0, 0)),
                  pl.BlockSpec((d, width), lambda i: (0, 0))],
        out_specs=pl.BlockSpec((1, n_mem, width), lambda i: (i, 0, 0)),
        out_shape=jax.ShapeDtypeStruct((b, n_mem, width), jnp.bfloat16),
        name="mem_kv",
    )(mem, g_mem, w_mem_kv)


def _rope_block(p, cos_t, sin_t):
    lane = lax.broadcasted_iota(jnp.int32, p.shape, 1) % DF_QK_DIM
    half = DF_ROT_DIMS // 2
    partner = jnp.where(lane < half, pltpu.roll(p, LANES - half, 1), pltpu.roll(p, half, 1))
    return p * cos_t + partner * sin_t


def _proj_kernel(x_ref, g_ref, w_ref, cos_ref, sin_ref, out_ref, vt_ref, *, layout, vt_cols):
    h = _rms(x_ref[...], g_ref[...]).astype(jnp.bfloat16)
    n_chunks = w_ref.shape[1] // PROJ_COL_CHUNK
    for c in range(n_chunks):
        c0 = c * PROJ_COL_CHUNK
        p = jnp.dot(h, w_ref[:, c0:c0 + PROJ_COL_CHUNK], preferred_element_type=jnp.float32)
        for j in range(PROJ_COL_CHUNK // LANES):
            col = c0 + j * LANES
            blk = p[:, j * LANES:(j + 1) * LANES]
            if vt_cols[0] <= col < vt_cols[1]:
                vt_ref[col - vt_cols[0]:col - vt_cols[0] + LANES, :] = blk.T.astype(vt_ref.dtype)
            scale, rope = layout(col)
            if rope:
                blk = _rope_block(blk, cos_ref[...], sin_ref[...])
            if scale != 1.0:
                blk = blk * scale
            out_ref[:, col:col + LANES] = blk.astype(out_ref.dtype)


def _proj(x2, g_pre, w_in, cos_t, sin_t, layout, vt_cols):
    n, d = x2.shape
    width = w_in.shape[1]
    v_width = vt_cols[1] - vt_cols[0]
    return pl.pallas_call(
        functools.partial(_proj_kernel, layout=layout, vt_cols=vt_cols),
        grid=(n // PROJ_ROWS,),
        in_specs=[pl.BlockSpec((PROJ_ROWS, d), lambda i: (i, 0)),
                  pl.BlockSpec((1, d), lambda i: (0, 0)),
                  pl.BlockSpec((d, width), lambda i: (0, 0)),
                  pl.BlockSpec((PROJ_ROWS, LANES), lambda i: (i, 0)),
                  pl.BlockSpec((PROJ_ROWS, LANES), lambda i: (i, 0))],
        out_specs=[pl.BlockSpec((PROJ_ROWS, width), lambda i: (i, 0)),
                   pl.BlockSpec((v_width, PROJ_ROWS), lambda i: (0, i))],
        out_shape=[jax.ShapeDtypeStruct((n, width), jnp.bfloat16),
                   jax.ShapeDtypeStruct((v_width, n), jnp.bfloat16)],
        compiler_params=pltpu.CompilerParams(vmem_limit_bytes=VMEM_LIMIT),
        name="proj",
    )(x2, g_pre, w_in, cos_t, sin_t)


def _log_one_minus_beta(nz):
    return jnp.minimum(nz, 0.0) - jnp.log(1.0 + jnp.exp2(jnp.abs(nz) * (-LOG2E)))


def _sb_kernel(q_ref, k_ref, v_ref, out_ref):
    t = ATT_BLOCK
    first_q = pl.program_id(2) * SB_Q_BLOCKS
    lane = lax.broadcasted_iota(jnp.int32, (t, LANES), 1)
    head_lanes = [lane < SB_HEAD_DIM, lane >= SB_HEAD_DIM]
    qs = []
    for j in range(SB_Q_BLOCKS):
        q = q_ref[j * t:(j + 1) * t, :]
        qs.append([jnp.where(m, q, jnp.zeros_like(q)) for m in head_lanes])
    row = lax.broadcasted_iota(jnp.int32, (t, t), 0)
    col = lax.broadcasted_iota(jnp.int32, (t, t), 1)
    tri = (row >= col).astype(jnp.bfloat16)
    tri2 = jnp.concatenate([tri, tri], axis=0)
    strict = col < row

    def run(blocks, carry):
        ks = [k_ref[pl.ds(pl.multiple_of(kb * t, t), t), :] for _, kb, _ in blocks]
        vs = [v_ref[pl.ds(pl.multiple_of(kb * t, t), t), :] for _, kb, _ in blocks]
        nz = [[lax.dot_general(qs[j][h], k, _NT, preferred_element_type=jnp.float32) for h in range(2)]
              for (j, _, _), k in zip(blocks, ks)]
        local = []
        for (_, _, masked), nz_b in zip(blocks, nz):
            parts = []
            for h in range(2):
                lg = _log_one_minus_beta(nz_b[h])
                if masked:
                    lg = jnp.where(strict, lg, 0.0)
                hi = lg.astype(jnp.bfloat16)
                lo = (lg - hi.astype(jnp.float32)).astype(jnp.bfloat16)
                parts.append(jnp.concatenate([hi, lo], axis=1))
            local.append([jnp.dot(hl, tri2, preferred_element_type=jnp.float32) for hl in parts])
        carry = {j: list(c) for j, c in carry.items()}
        ws = []
        for (j, _, masked), nz_b, local_b in zip(blocks, nz, local):
            w_b = []
            for h in range(2):
                c = local_b[h]
                total = jnp.broadcast_to(c[:, :1], (t, LANES))
                if carry[j][h] is not None:
                    c = c + jnp.concatenate([carry[j][h]] * (t // LANES), axis=1)
                    total = total + carry[j][h]
                w = jnp.exp(c - nz_b[h])
                if masked:
                    w = jnp.where(strict, w, 0.0)
                w_b.append(w.astype(jnp.bfloat16))
                carry[j][h] = total
            ws.append(w_b)
        pvs = []
        for w_b, v in zip(ws, vs):
            pvs.append(sum(jnp.dot(w_b[h], jnp.where(head_lanes[h], v, jnp.zeros_like(v)),
                                   preferred_element_type=jnp.float32) for h in range(2)))
        return pvs, carry

    def live(c):
        return (jnp.max(jnp.maximum(c[0], c[1])) > SB_DEAD_LOG).astype(jnp.int32)

    blocks = []
    for j in range(SB_Q_BLOCKS):
        blocks += [(j, first_q + j, True), (j, jnp.maximum(first_q + j - 1, 0), False)]
    pvs, carry = run(blocks, {j: (None, None) for j in range(SB_Q_BLOCKS)})

    for j in range(SB_Q_BLOCKS):
        pv_prev = pvs[2 * j + 1] if j > 0 else jnp.where(first_q > 0, pvs[1], 0.0)

        def step(loop_state, j=j):
            kb, _, acc, c0, c1 = loop_state
            (pv,), c = run([(j, kb, False)], {j: (c0, c1)})
            return kb - 1, live(c[j]), acc + pv, c[j][0], c[j][1]

        state = lax.while_loop(lambda s: (s[0] >= 0) & (s[1] > 0), step,
                               (first_q + j - 2, live(carry[j]), pvs[2 * j] + pv_prev,
                                carry[j][0], carry[j][1]))
        out_ref[j * t:(j + 1) * t, :] = state[2]


def _sb_attn(proj, batch, seq, q_col, k_col, v_col, n_pairs):
    t = ATT_BLOCK * SB_Q_BLOCKS
    nq = seq // t
    return pl.pallas_call(
        _sb_kernel,
        grid=(batch, n_pairs, nq),
        in_specs=[pl.BlockSpec((t, LANES), lambda b, p, i: (b * nq + i, q_col + p)),
                  pl.BlockSpec((seq, LANES), lambda b, p, i: (b, k_col + p)),
                  pl.BlockSpec((seq, LANES), lambda b, p, i: (b, v_col + p))],
        out_specs=pl.BlockSpec((t, LANES), lambda b, p, i: (b * nq + i, p)),
        out_shape=jax.ShapeDtypeStruct((batch * seq, n_pairs * LANES), jnp.float32),
        compiler_params=pltpu.CompilerParams(vmem_limit_bytes=VMEM_LIMIT),
        name="sb_attn",
    )(proj, proj, proj)


def _df_kernel(q_ref, k_ref, vt_ref, lam_ref, g_ref, out_ref, acc_ref, sc0_ref, sc1_ref, mrun_ref, mblk_ref,
               *, lambda_init):
    t = ATT_BLOCK
    sc_refs = (sc0_ref, sc1_ref)
    qi = pl.program_id(1)
    n_pairs = q_ref.shape[1] // LANES
    n_streams = 4 * n_pairs
    lane = lax.broadcasted_iota(jnp.int32, (t, LANES), 1)
    qs = []
    for s in range(n_streams):
        q = q_ref[:, (s // 4) * LANES:(s // 4 + 1) * LANES]
        qs.append(jnp.where(lane // DF_QK_DIM == s % 4, q, jnp.zeros_like(q)))
    key = lax.broadcasted_iota(jnp.int32, (t, t), 0)
    qry = lax.broadcasted_iota(jnp.int32, (t, t), 1)
    allowed = (key // CHUNK) <= (qry // CHUNK)
    ones = jnp.ones((DF_ONES_ROWS, t), jnp.bfloat16)

    def score(kb, slot, masked):
        start = pl.multiple_of(kb * t, t)
        k = k_ref[pl.ds(start, t), :]
        for s in range(n_streams):
            sc = lax.dot_general(k[:, (s // 4) * LANES:(s // 4 + 1) * LANES], qs[s], _NT,
                                 preferred_element_type=jnp.float32)
            if masked:
                sc = jnp.where(allowed, sc, -jnp.inf)
            sc_refs[slot][s] = sc
            mblk_ref[slot, s] = jnp.max(sc, axis=0, keepdims=True)

    def value(kb, slot):
        start = pl.multiple_of(kb * t, t)
        vt = vt_ref[:, pl.ds(start, t)]
        vaug = [jnp.concatenate([vt[h * DF_V_DIM:(h + 1) * DF_V_DIM], ones], axis=0)
                for h in range(n_streams // 2)]
        m_run = [mrun_ref[s] for s in range(n_streams)]
        m_new = [jnp.maximum(m_run[s], mblk_ref[slot, s]) for s in range(n_streams)]
        ps = [jnp.exp2(sc_refs[slot][s] - m_new[s]).astype(jnp.bfloat16) for s in range(n_streams)]
        for s in range(n_streams):
            pv = jnp.dot(vaug[s // 2], ps[s], preferred_element_type=jnp.float32)
            acc_ref[s] = acc_ref[s] * jnp.exp2(m_run[s] - m_new[s]) + pv
            mrun_ref[s] = m_new[s]

    def chain(kb, n):
        for i in range(n):
            score(jnp.maximum(kb - i - 1, 0), (i + 1) % 2, False)
            value(kb - i, i % 2)

    acc_ref[...] = jnp.zeros(acc_ref.shape, jnp.float32)
    mrun_ref[...] = jnp.full(mrun_ref.shape, -jnp.inf, jnp.float32)
    score(qi, 0, True)
    n_blocks = qi + 1
    n_trips = n_blocks // DF_UNROLL

    def trip(p, carry):
        chain(qi - DF_UNROLL * p, DF_UNROLL)
        return carry

    lax.fori_loop(0, n_trips, trip, 0)
    left = n_blocks - DF_UNROLL * n_trips
    size = DF_UNROLL // 2
    while size >= 2:
        @pl.when(left % (2 * size) >= size)
        def _(size=size):
            chain(left % (2 * size) - 1, size)
        size //= 2

    @pl.when(left % 2 == 1)
    def _():
        value(0, 0)

    lp = lam_ref[...]
    lam = (jnp.exp(jnp.sum(lp[0:1] * lp[1:2], axis=1, keepdims=True))
           - jnp.exp(jnp.sum(lp[2:3] * lp[3:4], axis=1, keepdims=True)) + lambda_init)
    heads = []
    for h in range(n_streams // 2):
        o = []
        for m in range(2):
            acc = acc_ref[2 * h + m]
            o.append(acc[:DF_V_DIM] / acc[DF_V_DIM:DF_V_DIM + 1])
        oh = o[0] - lam * o[1]
        ms = jnp.mean(oh * oh, axis=0, keepdims=True)
        heads.append(oh * lax.rsqrt(ms + EPS))
    for p in range(n_pairs):
        y = jnp.concatenate(heads[2 * p:2 * p + 2], axis=0).T
        out_ref[:, p * LANES:(p + 1) * LANES] = (y * g_ref[:, p * LANES:(p + 1) * LANES]
                                                 * (1.0 - lambda_init))


def _df_attn(proj, v_t, df_lambda, g_sub, batch, seq, q_col, k_col, lambda_init):
    t = ATT_BLOCK
    nq = seq // t
    width = v_t.shape[0]
    n_streams = 4 * (width // LANES)
    assert q_col % width == 0 and k_col % width == 0
    return pl.pallas_call(
        functools.partial(_df_kernel, lambda_init=lambda_init),
        grid=(batch, nq),
        in_specs=[pl.BlockSpec((t, width), lambda b, i: (b * nq + i, q_col // width)),
                  pl.BlockSpec((seq, width), lambda b, i: (b, k_col // width)),
                  pl.BlockSpec((width, seq), lambda b, i: (0, b)),
                  pl.BlockSpec(df_lambda.shape, lambda b, i: (0, 0)),
                  pl.BlockSpec((1, width), lambda b, i: (0, 0))],
        out_specs=pl.BlockSpec((t, width), lambda b, i: (b * nq + i, 0)),
        out_shape=jax.ShapeDtypeStruct((batch * seq, width), jnp.float32),
        scratch_shapes=[pltpu.VMEM((n_streams, DF_V_DIM + DF_ONES_ROWS, t), jnp.float32),
                        pltpu.VMEM((n_streams, t, t), jnp.float32),
                        pltpu.VMEM((n_streams, t, t), jnp.float32),
                        pltpu.VMEM((n_streams, 1, t), jnp.float32),
                        pltpu.VMEM((2, n_streams, 1, t), jnp.float32)],
        compiler_params=pltpu.CompilerParams(vmem_limit_bytes=VMEM_LIMIT),
        name="df_attn",
    )(proj, proj, v_t, df_lambda, g_sub)


def _silu(g):
    return g * (1.0 / (1.0 + jnp.exp(-g)))


def _merge_kernel(x_ref, ysb_ref, ydf_ref, gsb_ref, gdf_ref, mq_ref, gm_ref, mkv_ref,
                  w_ref, g_ref, out_ref):
    width = mq_ref.shape[1]
    kv = mkv_ref[0]
    km = kv[:, :width]
    vm = kv[:, width:]
    sub = MERGE_SUB_ROWS
    lane_q = lax.broadcasted_iota(jnp.int32, (sub, width), 1) // MEM_HEAD_DIM
    lane_v = lax.broadcasted_iota(jnp.int32, vm.shape, 1) // MEM_HEAD_DIM
    vms = [jnp.where(lane_v == h, vm, jnp.zeros_like(vm)) for h in range(MEM_HEADS)]
    rows = [pl.ds(i * sub, sub) for i in range(mq_ref.shape[0] // sub)]
    scs = []
    for r in rows:
        mq = mq_ref[r, :]
        scs.append([lax.dot_general(jnp.where(lane_q == h, mq, jnp.zeros_like(mq)), km, _NT,
                                    preferred_element_type=jnp.float32) for h in range(MEM_HEADS)])
    ps = []
    for sc_r in scs:
        p_r = []
        for sc in sc_r:
            p = jnp.exp2(sc - jnp.max(sc, axis=1, keepdims=True))
            p_r.append((p * (1.0 / jnp.sum(p, axis=1, keepdims=True))).astype(jnp.bfloat16))
        ps.append(p_r)
    y_ms = [sum(jnp.dot(p_r[h], vms[h], preferred_element_type=jnp.float32) for h in range(MEM_HEADS))
            for p_r in ps]
    ys = []
    for r, y_m in zip(rows, y_ms):
        ys.append(jnp.concatenate([ysb_ref[r, :] * _silu(gsb_ref[r, :].astype(jnp.float32)),
                                   ydf_ref[r, :] * _silu(gdf_ref[r, :].astype(jnp.float32)),
                                   y_m * _silu(gm_ref[r, :].astype(jnp.float32))], axis=1).astype(jnp.bfloat16))
    os_ = [jnp.dot(y, w_ref[...], preferred_element_type=jnp.float32) for y in ys]
    for r, o in zip(rows, os_):
        out_ref[r, :] = x_ref[r, :] + _rms(o, g_ref[...])


def _merge(x2, y_sb, y_df, proj, mkv, w_out, g_post, seq, cols):
    n, d = x2.shape
    r = MERGE_ROWS
    sb_w, df_w, m_w = y_sb.shape[1], y_df.shape[1], mkv.shape[2] // 2
    gsb_col, gdf_col, mq_col, gm_col = cols
    per_batch = seq // r
    return pl.pallas_call(
        _merge_kernel,
        grid=(n // r,),
        in_specs=[pl.BlockSpec((r, d), lambda i: (i, 0)),
                  pl.BlockSpec((r, sb_w), lambda i: (i, 0)),
                  pl.BlockSpec((r, df_w), lambda i: (i, 0)),
                  pl.BlockSpec((r, sb_w), lambda i: (i, gsb_col // sb_w)),
                  pl.BlockSpec((r, df_w), lambda i: (i, gdf_col // df_w)),
                  pl.BlockSpec((r, m_w), lambda i: (i, mq_col // m_w)),
                  pl.BlockSpec((r, m_w), lambda i: (i, gm_col // m_w)),
                  pl.BlockSpec((1,) + mkv.shape[1:], lambda i: (i // per_batch, 0, 0)),
                  pl.BlockSpec(w_out.shape, lambda i: (0, 0)),
                  pl.BlockSpec((1, d), lambda i: (0, 0))],
        out_specs=pl.BlockSpec((r, d), lambda i: (i, 0)),
        out_shape=jax.ShapeDtypeStruct((n, d), jnp.float32),
        compiler_params=pltpu.CompilerParams(vmem_limit_bytes=VMEM_LIMIT),
        name="merge",
    )(x2, y_sb, y_df, proj, proj, proj, proj, mkv, w_out, g_post)


def _rope_tables(positions):
    half = DF_ROT_DIMS // 2
    inv_freq = 1.0 / (ROPE_THETA ** (jnp.arange(0, DF_ROT_DIMS, 2, dtype=jnp.float32) / DF_ROT_DIMS))
    ang = positions.astype(jnp.float32)[:, :, None] * inv_freq
    cos, sin = jnp.cos(ang), jnp.sin(ang)
    pad = DF_QK_DIM - DF_ROT_DIMS
    cos_g = jnp.concatenate([cos, cos, jnp.ones(cos.shape[:2] + (pad,), jnp.float32)], axis=-1)
    sin_g = jnp.concatenate([-sin, sin, jnp.zeros(sin.shape[:2] + (pad,), jnp.float32)], axis=-1)
    reps = LANES // DF_QK_DIM
    b, s = positions.shape
    return (jnp.tile(cos_g, (1, 1, reps)).reshape(b * s, LANES),
            jnp.tile(sin_g, (1, 1, reps)).reshape(b * s, LANES))


def kernel(x, mem, positions, w_in, w_mem_kv, w_out, g_pre, g_post, g_mem, g_subln, df_lambda):
    batch, seq, d = x.shape
    depth = w_in.shape[0]
    sb_w, df_w, m_w = d // 2, d // 4, d // 4
    sb_q, sb_k, sb_v, sb_g = 0, sb_w, 2 * sb_w, 3 * sb_w
    df_q = 4 * sb_w
    df_k, df_v, df_g = df_q + df_w, df_q + 2 * df_w, df_q + 3 * df_w
    m_q, m_g = df_q + 4 * df_w, df_q + 4 * df_w + m_w
    assert m_g + m_w == w_in.shape[2]
    assert seq % (ATT_BLOCK * SB_Q_BLOCKS) == 0 and (batch * seq) % PROJ_ROWS == 0 and seq % MERGE_ROWS == 0

    def layout(col):
        if sb_q <= col < sb_k:
            return -(SB_HEAD_DIM ** -0.5), False
        if df_q <= col < df_k:
            return DF_QK_DIM ** -0.5 * LOG2E, True
        if df_k <= col < df_v:
            return 1.0, True
        if m_q <= col < m_g:
            return MEM_HEAD_DIM ** -0.5 * LOG2E, False
        return 1.0, False

    cos_t, sin_t = _rope_tables(positions)
    x2 = x.reshape(batch * seq, d)
    for layer in range(depth):
        lambda_init = 0.8 - 0.6 * math.exp(-0.3 * layer)
        mkv = _mem_kv(mem, g_mem[layer][None], w_mem_kv[layer].astype(jnp.bfloat16))
        proj, v_t = _proj(x2, g_pre[layer][None], w_in[layer].astype(jnp.bfloat16), cos_t, sin_t, layout,
                          (df_v, df_g))
        y_sb = _sb_attn(proj, batch, seq, sb_q // LANES, sb_k // LANES, sb_v // LANES, sb_w // LANES)
        g_sub = jnp.tile(g_subln[layer], df_w // DF_V_DIM)[None]
        y_df = _df_attn(proj, v_t, df_lambda[layer], g_sub, batch, seq, df_q, df_k, lambda_init)
        x2 = _merge(x2, y_sb, y_df, proj, mkv, w_out[layer].astype(jnp.bfloat16), g_post[layer][None],
                    seq, (sb_g, df_g, m_q, m_g))
    return x2.reshape(batch, seq, d)
```

```python
import functools
import math

import jax
import jax.numpy as jnp
import numpy as np
from jax import lax
from jax.experimental import pallas as pl
from jax.experimental.pallas import tpu as pltpu

EPS = 1e-6
ROPE_THETA = 500000.0
CHUNK = 64

LANES = 128
SB_HEAD_DIM = 64
DF_QK_DIM = 32
DF_V_DIM = 64
DF_ROT_DIMS = 8
MEM_HEAD_DIM = 64
MEM_HEADS = 4
DF_ONES_ROWS = 16
DF_UNROLL = 4
LOG2E = 1.4426950408889634

ATT_BLOCK = 256
SB_Q_BLOCKS = 2
PROJ_ROWS = 512
MERGE_ROWS = 1024
MERGE_SUB_ROWS = 256
PROJ_COL_CHUNK = 512
VMEM_LIMIT = 48 * 1024 * 1024
SB_DEAD_LOG = -110.0

_NT = (((1,), (1,)), ((), ()))


def _rms(xf, g):
    return xf * lax.rsqrt(jnp.mean(xf * xf, axis=-1, keepdims=True) + EPS) * g


def _mem_kv_kernel(mem_ref, g_ref, w_ref, out_ref):
    h = _rms(mem_ref[0], g_ref[...]).astype(jnp.bfloat16)
    out_ref[0] = jnp.dot(h, w_ref[...], preferred_element_type=jnp.float32).astype(out_ref.dtype)


def _mem_kv(mem, g_mem, w_mem_kv):
    b, n_mem, d = mem.shape
    width = w_mem_kv.shape[1]
    return pl.pallas_call(
        _mem_kv_kernel,
        grid=(b,),
        in_specs=[pl.BlockSpec((1, n_mem, d), lambda i: (i, 0, 0)),
                  pl.BlockSpec((1, d), lambda i: (0, 0)),
                  pl.BlockSpec((d, width), lambda i: (0, 0))],
        out_specs=pl.BlockSpec((1, n_mem, width), lambda i: (i, 0, 0)),
        out_shape=jax.ShapeDtypeStruct((b, n_mem, width), jnp.bfloat16),
        name="mem_kv",
    )(mem, g_mem, w_mem_kv)


def _rope_block(p, cos_t, sin_t):
    lane = lax.broadcasted_iota(jnp.int32, p.shape, 1) % DF_QK_DIM
    half = DF_ROT_DIMS // 2
    partner = jnp.where(lane < half, pltpu.roll(p, LANES - half, 1), pltpu.roll(p, half, 1))
    return p * cos_t + partner * sin_t


def _proj_kernel(x_ref, g_ref, w_ref, cos_ref, sin_ref, out_ref, vt_ref, *, layout, vt_cols):
    h = _rms(x_ref[...], g_ref[...]).astype(jnp.bfloat16)
    n_chunks = w_ref.shape[1] // PROJ_COL_CHUNK
    for c in range(n_chunks):
        c0 = c * PROJ_COL_CHUNK
        p = jnp.dot(h, w_ref[:, c0:c0 + PROJ_COL_CHUNK], preferred_element_type=jnp.float32)
        for j in range(PROJ_COL_CHUNK // LANES):
            col = c0 + j * LANES
            blk = p[:, j * LANES:(j + 1) * LANES]
            if vt_cols[0] <= col < vt_cols[1]:
                vt_ref[col - vt_cols[0]:col - vt_cols[0] + LANES, :] = blk.T.astype(vt_ref.dtype)
            scale, rope = layout(col)
            if rope:
                blk = _rope_block(blk, cos_ref[...], sin_ref[...])
            if scale != 1.0:
                blk = blk * scale
            out_ref[:, col:col + LANES] = blk.astype(out_ref.dtype)


def _proj(x2, g_pre, w_in, rope_tab, layout, vt_cols):
    n, d = x2.shape
    width = w_in.shape[1]
    v_width = vt_cols[1] - vt_cols[0]
    return pl.pallas_call(
        functools.partial(_proj_kernel, layout=layout, vt_cols=vt_cols),
        grid=(n // PROJ_ROWS,),
        in_specs=[pl.BlockSpec((PROJ_ROWS, d), lambda i: (i, 0)),
                  pl.BlockSpec((1, d), lambda i: (0, 0)),
                  pl.BlockSpec((d, width), lambda i: (0, 0)),
                  pl.BlockSpec((PROJ_ROWS, LANES), lambda i: (i, 0)),
                  pl.BlockSpec((PROJ_ROWS, LANES), lambda i: (i, 1))],
        out_specs=[pl.BlockSpec((PROJ_ROWS, width), lambda i: (i, 0)),
                   pl.BlockSpec((v_width, PROJ_ROWS), lambda i: (0, i))],
        out_shape=[jax.ShapeDtypeStruct((n, width), jnp.bfloat16),
                   jax.ShapeDtypeStruct((v_width, n), jnp.bfloat16)],
        compiler_params=pltpu.CompilerParams(vmem_limit_bytes=VMEM_LIMIT),
        name="proj",
    )(x2, g_pre, w_in, rope_tab, rope_tab)


def _log_one_minus_beta(nz):
    return jnp.minimum(nz, 0.0) - jnp.log(1.0 + jnp.exp2(jnp.abs(nz) * (-LOG2E)))


def _sb_kernel(q_ref, k_ref, v_ref, out_ref):
    t = ATT_BLOCK
    first_q = pl.program_id(2) * SB_Q_BLOCKS
    lane = lax.broadcasted_iota(jnp.int32, (t, LANES), 1)
    head_lanes = [lane < SB_HEAD_DIM, lane >= SB_HEAD_DIM]
    qs = []
    for j in range(SB_Q_BLOCKS):
        q = q_ref[j * t:(j + 1) * t, :]
        qs.append([jnp.where(m, q, jnp.zeros_like(q)) for m in head_lanes])
    row = lax.broadcasted_iota(jnp.int32, (t, t), 0)
    col = lax.broadcasted_iota(jnp.int32, (t, t), 1)
    tri = (row >= col).astype(jnp.bfloat16)
    tri2 = jnp.concatenate([tri, tri], axis=0)
    strict = col < row

    def run(blocks, carry):
        ks = [k_ref[pl.ds(pl.multiple_of(kb * t, t), t), :] for _, kb, _ in blocks]
        vs = [v_ref[pl.ds(pl.multiple_of(kb * t, t), t), :] for _, kb, _ in blocks]
        nz = [[lax.dot_general(qs[j][h], k, _NT, preferred_element_type=jnp.float32) for h in range(2)]
              for (j, _, _), k in zip(blocks, ks)]
        local = []
        for (_, _, masked), nz_b in zip(blocks, nz):
            parts = []
            for h in range(2):
                lg = _log_one_minus_beta(nz_b[h])
                if masked:
                    lg = jnp.where(strict, lg, 0.0)
                hi = lg.astype(jnp.bfloat16)
                lo = (lg - hi.astype(jnp.float32)).astype(jnp.bfloat16)
                parts.append(jnp.concatenate([hi, lo], axis=1))
            local.append([jnp.dot(hl, tri2, preferred_element_type=jnp.float32) for hl in parts])
        carry = {j: list(c) for j, c in carry.items()}
        ws = []
        for (j, _, masked), nz_b, local_b in zip(blocks, nz, local):
            w_b = []
            for h in range(2):
                c = local_b[h]
                total = jnp.broadcast_to(c[:, :1], (t, LANES))
                if carry[j][h] is not None:
                    c = c + jnp.concatenate([carry[j][h]] * (t // LANES), axis=1)
                    total = total + carry[j][h]
                w = jnp.exp(c - nz_b[h])
                if masked:
                    w = jnp.where(strict, w, 0.0)
                w_b.append(w.astype(jnp.bfloat16))
                carry[j][h] = total
            ws.append(w_b)
        pvs = []
        for w_b, v in zip(ws, vs):
            pvs.append(sum(jnp.dot(w_b[h], jnp.where(head_lanes[h], v, jnp.zeros_like(v)),
                                   preferred_element_type=jnp.float32) for h in range(2)))
        return pvs, carry

    def live(c):
        return (jnp.max(jnp.maximum(c[0], c[1])) > SB_DEAD_LOG).astype(jnp.int32)

    blocks = []
    for j in range(SB_Q_BLOCKS):
        blocks += [(j, first_q + j, True), (j, jnp.maximum(first_q + j - 1, 0), False)]
    pvs, carry = run(blocks, {j: (None, None) for j in range(SB_Q_BLOCKS)})

    for j in range(SB_Q_BLOCKS):
        pv_prev = pvs[2 * j + 1] if j > 0 else jnp.where(first_q > 0, pvs[1], 0.0)

        def step(loop_state, j=j):
            kb, _, acc, c0, c1 = loop_state
            (pv,), c = run([(j, kb, False)], {j: (c0, c1)})
            return kb - 1, live(c[j]), acc + pv, c[j][0], c[j][1]

        state = lax.while_loop(lambda s: (s[0] >= 0) & (s[1] > 0), step,
                               (first_q + j - 2, live(carry[j]), pvs[2 * j] + pv_prev,
                                carry[j][0], carry[j][1]))
        out_ref[j * t:(j + 1) * t, :] = state[2]


def _sb_attn(proj, batch, seq, q_col, k_col, v_col, n_pairs):
    t = ATT_BLOCK * SB_Q_BLOCKS
    nq = seq // t
    return pl.pallas_call(
        _sb_kernel,
        grid=(batch, n_pairs, nq),
        in_specs=[pl.BlockSpec((t, LANES), lambda b, p, i: (b * nq + i, q_col + p)),
                  pl.BlockSpec((seq, LANES), lambda b, p, i: (b, k_col + p)),
                  pl.BlockSpec((seq, LANES), lambda b, p, i: (b, v_col + p))],
        out_specs=pl.BlockSpec((t, LANES), lambda b, p, i: (b * nq + i, p)),
        out_shape=jax.ShapeDtypeStruct((batch * seq, n_pairs * LANES), jnp.float32),
        compiler_params=pltpu.CompilerParams(vmem_limit_bytes=VMEM_LIMIT),
        name="sb_attn",
    )(proj, proj, proj)


def _df_kernel(q_ref, k_ref, vt_ref, lam_ref, g_ref, out_ref, acc_ref, sc0_ref, sc1_ref, mrun_ref, mblk_ref,
               *, lambda_init):
    t = ATT_BLOCK
    sc_refs = (sc0_ref, sc1_ref)
    qi = pl.program_id(1)
    n_pairs = q_ref.shape[1] // LANES
    n_streams = 4 * n_pairs
    lane = lax.broadcasted_iota(jnp.int32, (t, LANES), 1)
    qs = []
    for s in range(n_streams):
        q = q_ref[:, (s // 4) * LANES:(s // 4 + 1) * LANES]
        qs.append(jnp.where(lane // DF_QK_DIM == s % 4, q, jnp.zeros_like(q)))
    key = lax.broadcasted_iota(jnp.int32, (t, t), 0)
    qry = lax.broadcasted_iota(jnp.int32, (t, t), 1)
    allowed = (key // CHUNK) <= (qry // CHUNK)
    ones = jnp.ones((DF_ONES_ROWS, t), jnp.bfloat16)

    def score(kb, slot, masked):
        start = pl.multiple_of(kb * t, t)
        k = k_ref[pl.ds(start, t), :]
        for s in range(n_streams):
            sc = lax.dot_general(k[:, (s // 4) * LANES:(s // 4 + 1) * LANES], qs[s], _NT,
                                 preferred_element_type=jnp.float32)
            if masked:
                sc = jnp.where(allowed, sc, -jnp.inf)
            sc_refs[slot][s] = sc
            mblk_ref[slot, s] = jnp.max(sc, axis=0, keepdims=True)

    def value(kb, slot):
        start = pl.multiple_of(kb * t, t)
        vt = vt_ref[:, pl.ds(start, t)]
        vaug = [jnp.concatenate([vt[h * DF_V_DIM:(h + 1) * DF_V_DIM], ones], axis=0)
                for h in range(n_streams // 2)]
        m_run = [mrun_ref[s] for s in range(n_streams)]
        m_new = [jnp.maximum(m_run[s], mblk_ref[slot, s]) for s in range(n_streams)]
        ps = [jnp.exp2(sc_refs[slot][s] - m_new[s]).astype(jnp.bfloat16) for s in range(n_streams)]
        for s in range(n_streams):
            pv = jnp.dot(vaug[s // 2], ps[s], preferred_element_type=jnp.float32)
            acc_ref[s] = acc_ref[s] * jnp.exp2(m_run[s] - m_new[s]) + pv
            mrun_ref[s] = m_new[s]

    def chain(kb, n):
        for i in range(n):
            score(jnp.maximum(kb - i - 1, 0), (i + 1) % 2, False)
            value(kb - i, i % 2)

    acc_ref[...] = jnp.zeros(acc_ref.shape, jnp.float32)
    mrun_ref[...] = jnp.full(mrun_ref.shape, -jnp.inf, jnp.float32)
    score(qi, 0, True)
    n_blocks = qi + 1
    n_trips = n_blocks // DF_UNROLL

    def trip(p, carry):
        chain(qi - DF_UNROLL * p, DF_UNROLL)
        return carry

    lax.fori_loop(0, n_trips, trip, 0)
    left = n_blocks - DF_UNROLL * n_trips
    size = DF_UNROLL // 2
    while size >= 2:
        @pl.when(left % (2 * size) >= size)
        def _(size=size):
            chain(left % (2 * size) - 1, size)
        size //= 2

    @pl.when(left % 2 == 1)
    def _():
        value(0, 0)

    lp = lam_ref[...]
    lam = (jnp.exp(jnp.sum(lp[0:1] * lp[1:2], axis=1, keepdims=True))
           - jnp.exp(jnp.sum(lp[2:3] * lp[3:4], axis=1, keepdims=True)) + lambda_init)
    heads = []
    for h in range(n_streams // 2):
        o = []
        for m in range(2):
            acc = acc_ref[2 * h + m]
            o.append(acc[:DF_V_DIM] / acc[DF_V_DIM:DF_V_DIM + 1])
        oh = o[0] - lam * o[1]
        ms = jnp.mean(oh * oh, axis=0, keepdims=True)
        heads.append(oh * lax.rsqrt(ms + EPS))
    for p in range(n_pairs):
        y = jnp.concatenate(heads[2 * p:2 * p + 2], axis=0).T
        out_ref[:, p * LANES:(p + 1) * LANES] = (y * g_ref[:, p * LANES:(p + 1) * LANES]
                                                 * (1.0 - lambda_init))


def _df_attn(proj, v_t, df_lambda, g_sub, batch, seq, q_col, k_col, lambda_init):
    t = ATT_BLOCK
    nq = seq // t
    width = v_t.shape[0]
    n_streams = 4 * (width // LANES)
    assert q_col % width == 0 and k_col % width == 0
    return pl.pallas_call(
        functools.partial(_df_kernel, lambda_init=lambda_init),
        grid=(batch, nq),
        in_specs=[pl.BlockSpec((t, width), lambda b, i: (b * nq + i, q_col // width)),
                  pl.BlockSpec((seq, width), lambda b, i: (b, k_col // width)),
                  pl.BlockSpec((width, seq), lambda b, i: (0, b)),
                  pl.BlockSpec(df_lambda.shape, lambda b, i: (0, 0)),
                  pl.BlockSpec((1, width), lambda b, i: (0, 0))],
        out_specs=pl.BlockSpec((t, width), lambda b, i: (b * nq + i, 0)),
        out_shape=jax.ShapeDtypeStruct((batch * seq, width), jnp.float32),
        scratch_shapes=[pltpu.VMEM((n_streams, DF_V_DIM + DF_ONES_ROWS, t), jnp.float32),
                        pltpu.VMEM((n_streams, t, t), jnp.float32),
                        pltpu.VMEM((n_streams, t, t), jnp.float32),
                        pltpu.VMEM((n_streams, 1, t), jnp.float32),
                        pltpu.VMEM((2, n_streams, 1, t), jnp.float32)],
        compiler_params=pltpu.CompilerParams(vmem_limit_bytes=VMEM_LIMIT),
        name="df_attn",
    )(proj, proj, v_t, df_lambda, g_sub)


def _silu(g):
    return g * (1.0 / (1.0 + jnp.exp(-g)))


def _merge_kernel(x_ref, ysb_ref, ydf_ref, gsb_ref, gdf_ref, mq_ref, gm_ref, mkv_ref,
                  w_ref, g_ref, out_ref):
    width = mq_ref.shape[1]
    kv = mkv_ref[0]
    km = kv[:, :width]
    vm = kv[:, width:]
    sub = MERGE_SUB_ROWS
    lane_q = lax.broadcasted_iota(jnp.int32, (sub, width), 1) // MEM_HEAD_DIM
    lane_v = lax.broadcasted_iota(jnp.int32, vm.shape, 1) // MEM_HEAD_DIM
    vms = [jnp.where(lane_v == h, vm, jnp.zeros_like(vm)) for h in range(MEM_HEADS)]
    rows = [pl.ds(i * sub, sub) for i in range(mq_ref.shape[0] // sub)]
    scs = []
    for r in rows:
        mq = mq_ref[r, :]
        scs.append([lax.dot_general(jnp.where(lane_q == h, mq, jnp.zeros_like(mq)), km, _NT,
                                    preferred_element_type=jnp.float32) for h in range(MEM_HEADS)])
    ps = []
    for sc_r in scs:
        p_r = []
        for sc in sc_r:
            p = jnp.exp2(sc - jnp.max(sc, axis=1, keepdims=True))
            p_r.append((p * (1.0 / jnp.sum(p, axis=1, keepdims=True))).astype(jnp.bfloat16))
        ps.append(p_r)
    y_ms = [sum(jnp.dot(p_r[h], vms[h], preferred_element_type=jnp.float32) for h in range(MEM_HEADS))
            for p_r in ps]
    ys = []
    for r, y_m in zip(rows, y_ms):
        ys.append(jnp.concatenate([ysb_ref[r, :] * _silu(gsb_ref[r, :].astype(jnp.float32)),
                                   ydf_ref[r, :] * _silu(gdf_ref[r, :].astype(jnp.float32)),
                                   y_m * _silu(gm_ref[r, :].astype(jnp.float32))], axis=1).astype(jnp.bfloat16))
    os_ = [jnp.dot(y, w_ref[...], preferred_element_type=jnp.float32) for y in ys]
    for r, o in zip(rows, os_):
        out_ref[r, :] = x_ref[r, :] + _rms(o, g_ref[...])


def _merge(x2, y_sb, y_df, proj, mkv, w_out, g_post, seq, cols):
    n, d = x2.shape
    r = MERGE_ROWS
    sb_w, df_w, m_w = y_sb.shape[1], y_df.shape[1], mkv.shape[2] // 2
    gsb_col, gdf_col, mq_col, gm_col = cols
    per_batch = seq // r
    return pl.pallas_call(
        _merge_kernel,
        grid=(n // r,),
        in_specs=[pl.BlockSpec((r, d), lambda i: (i, 0)),
                  pl.BlockSpec((r, sb_w), lambda i: (i, 0)),
                  pl.BlockSpec((r, df_w), lambda i: (i, 0)),
                  pl.BlockSpec((r, sb_w), lambda i: (i, gsb_col // sb_w)),
                  pl.BlockSpec((r, df_w), lambda i: (i, gdf_col // df_w)),
                  pl.BlockSpec((r, m_w), lambda i: (i, mq_col // m_w)),
                  pl.BlockSpec((r, m_w), lambda i: (i, gm_col // m_w)),
                  pl.BlockSpec((1,) + mkv.shape[1:], lambda i: (i // per_batch, 0, 0)),
                  pl.BlockSpec(w_out.shape, lambda i: (0, 0)),
                  pl.BlockSpec((1, d), lambda i: (0, 0))],
        out_specs=pl.BlockSpec((r, d), lambda i: (i, 0)),
        out_shape=jax.ShapeDtypeStruct((n, d), jnp.float32),
        compiler_params=pltpu.CompilerParams(vmem_limit_bytes=VMEM_LIMIT),
        name="merge",
    )(x2, y_sb, y_df, proj, proj, proj, proj, mkv, w_out, g_post)


def _rope_tables(positions):
    half = DF_ROT_DIMS // 2
    inv_freq = 1.0 / (ROPE_THETA ** (jnp.arange(0, DF_ROT_DIMS, 2, dtype=jnp.float32) / DF_ROT_DIMS))
    ang = positions.astype(jnp.float32)[:, :, None] * inv_freq
    b, s = positions.shape
    ones = jnp.ones((b, s, 1), jnp.float32)
    src = jnp.concatenate([jnp.cos(ang), jnp.sin(ang), ones], axis=-1).reshape(b * s, 2 * half + 1)
    d = np.arange(LANES) % DF_QK_DIM
    sel_cos = np.zeros((2 * half + 1, LANES), np.float32)
    sel_sin = np.zeros((2 * half + 1, LANES), np.float32)
    for lane, dd in enumerate(d):
        if dd < DF_ROT_DIMS:
            sel_cos[dd % half, lane] = 1.0
            sel_sin[half + dd % half, lane] = -1.0 if dd < half else 1.0
        else:
            sel_cos[2 * half, lane] = 1.0
    sel = jnp.asarray(np.concatenate([sel_cos, sel_sin], axis=1))
    return jnp.dot(src, sel, precision=lax.Precision.HIGHEST)


def kernel(x, mem, positions, w_in, w_mem_kv, w_out, g_pre, g_post, g_mem, g_subln, df_lambda):
    batch, seq, d = x.shape
    depth = w_in.shape[0]
    sb_w, df_w, m_w = d // 2, d // 4, d // 4
    sb_q, sb_k, sb_v, sb_g = 0, sb_w, 2 * sb_w, 3 * sb_w
    df_q = 4 * sb_w
    df_k, df_v, df_g = df_q + df_w, df_q + 2 * df_w, df_q + 3 * df_w
    m_q, m_g = df_q + 4 * df_w, df_q + 4 * df_w + m_w
    assert m_g + m_w == w_in.shape[2]
    assert seq % (ATT_BLOCK * SB_Q_BLOCKS) == 0 and (batch * seq) % PROJ_ROWS == 0 and seq % MERGE_ROWS == 0

    def layout(col):
        if sb_q <= col < sb_k:
            return -(SB_HEAD_DIM ** -0.5), False
        if df_q <= col < df_k:
            return DF_QK_DIM ** -0.5 * LOG2E, True
        if df_k <= col < df_v:
            return 1.0, True
        if m_q <= col < m_g:
            return MEM_HEAD_DIM ** -0.5 * LOG2E, False
        return 1.0, False

    rope_tab = _rope_tables(positions)
    x2 = x.reshape(batch * seq, d)
    for layer in range(depth):
        lambda_init = 0.8 - 0.6 * math.exp(-0.3 * layer)
        mkv = _mem_kv(mem, g_mem[layer][None], w_mem_kv[layer].astype(jnp.bfloat16))
        proj, v_t = _proj(x2, g_pre[layer][None], w_in[layer].astype(jnp.bfloat16), rope_tab, layout,
                          (df_v, df_g))
        y_sb = _sb_attn(proj, batch, seq, sb_q // LANES, sb_k // LANES, sb_v // LANES, sb_w // LANES)
        g_sub = jnp.tile(g_subln[layer], df_w // DF_V_DIM)[None]
        y_df = _df_attn(proj, v_t, df_lambda[layer], g_sub, batch, seq, df_q, df_k, lambda_init)
        x2 = _merge(x2, y_sb, y_df, proj, mkv, w_out[layer].astype(jnp.bfloat16), g_post[layer][None],
                    seq, (sb_g, df_g, m_q, m_g))
    return x2.reshape(batch, seq, d)
```

```python
import functools
import math

import jax
import jax.numpy as jnp
from jax import lax
from jax.experimental import pallas as pl
from jax.experimental.pallas import tpu as pltpu

EPS = 1e-6
ROPE_THETA = 500000.0
CHUNK = 64

LANES = 128
SB_HEAD_DIM = 64
DF_QK_DIM = 32
DF_V_DIM = 64
DF_ROT_DIMS = 8
MEM_HEAD_DIM = 64
MEM_HEADS = 4
DF_ONES_ROWS = 16
DF_UNROLL = 4
LOG2E = 1.4426950408889634

ATT_BLOCK = 256
SB_Q_BLOCKS = 4
PROJ_ROWS = 512
MERGE_ROWS = 1024
MERGE_SUB_ROWS = 256
PROJ_COL_CHUNK = 512
VMEM_LIMIT = 48 * 1024 * 1024
SB_DEAD_LOG = -110.0

_NT = (((1,), (1,)), ((), ()))


def _rms(xf, g):
    return xf * lax.rsqrt(jnp.mean(xf * xf, axis=-1, keepdims=True) + EPS) * g


def _mem_kv_kernel(mem_ref, g_ref, w_ref, out_ref):
    h = _rms(mem_ref[0], g_ref[...]).astype(jnp.bfloat16)
    out_ref[0] = jnp.dot(h, w_ref[...], preferred_element_type=jnp.float32).astype(out_ref.dtype)


def _mem_kv(mem, g_mem, w_mem_kv):
    b, n_mem, d = mem.shape
    width = w_mem_kv.shape[1]
    return pl.pallas_call(
        _mem_kv_kernel,
        grid=(b,),
        in_specs=[pl.BlockSpec((1, n_mem, d), lambda i: (i, 0, 0)),
                  pl.BlockSpec((1, d), lambda i: (0, 0)),
                  pl.BlockSpec((d, width), lambda i: (0, 0))],
        out_specs=pl.BlockSpec((1, n_mem, width), lambda i: (i, 0, 0)),
        out_shape=jax.ShapeDtypeStruct((b, n_mem, width), jnp.bfloat16),
        name="mem_kv",
    )(mem, g_mem, w_mem_kv)


def _rope_block(p, cos_t, sin_t):
    lane = lax.broadcasted_iota(jnp.int32, p.shape, 1) % DF_QK_DIM
    half = DF_ROT_DIMS // 2
    partner = jnp.where(lane < half, pltpu.roll(p, LANES - half, 1), pltpu.roll(p, half, 1))
    return p * cos_t + partner * sin_t


def _rope_lanes(cs):
    half = DF_ROT_DIMS // 2
    shape = (cs.shape[0], LANES)
    d = lax.broadcasted_iota(jnp.int32, shape, 1) % DF_QK_DIM
    cos_t = jnp.ones(shape, jnp.float32)
    sin_t = jnp.zeros(shape, jnp.float32)
    for i in range(half):
        c = jnp.broadcast_to(cs[:, i:i + 1], shape)
        s = jnp.broadcast_to(cs[:, half + i:half + i + 1], shape)
        cos_t = jnp.where((d == i) | (d == i + half), c, cos_t)
        sin_t = jnp.where(d == i, -s, jnp.where(d == i + half, s, sin_t))
    return cos_t, sin_t


def _proj_kernel(x_ref, g_ref, w_ref, cs_ref, out_ref, vt_ref, *, layout, vt_cols):
    h = _rms(x_ref[...], g_ref[...]).astype(jnp.bfloat16)
    cos_t, sin_t = _rope_lanes(cs_ref[...])
    n_chunks = w_ref.shape[1] // PROJ_COL_CHUNK
    for c in range(n_chunks):
        c0 = c * PROJ_COL_CHUNK
        p = jnp.dot(h, w_ref[:, c0:c0 + PROJ_COL_CHUNK], preferred_element_type=jnp.float32)
        for j in range(PROJ_COL_CHUNK // LANES):
            col = c0 + j * LANES
            blk = p[:, j * LANES:(j + 1) * LANES]
            if vt_cols[0] <= col < vt_cols[1]:
                vt_ref[col - vt_cols[0]:col - vt_cols[0] + LANES, :] = blk.T.astype(vt_ref.dtype)
            scale, rope = layout(col)
            if rope:
                blk = _rope_block(blk, cos_t, sin_t)
            if scale != 1.0:
                blk = blk * scale
            out_ref[:, col:col + LANES] = blk.astype(out_ref.dtype)


def _proj(x2, g_pre, w_in, rope_tab, layout, vt_cols):
    n, d = x2.shape
    width = w_in.shape[1]
    v_width = vt_cols[1] - vt_cols[0]
    return pl.pallas_call(
        functools.partial(_proj_kernel, layout=layout, vt_cols=vt_cols),
        grid=(n // PROJ_ROWS,),
        in_specs=[pl.BlockSpec((PROJ_ROWS, d), lambda i: (i, 0)),
                  pl.BlockSpec((1, d), lambda i: (0, 0)),
                  pl.BlockSpec((d, width), lambda i: (0, 0)),
                  pl.BlockSpec((PROJ_ROWS, rope_tab.shape[1]), lambda i: (i, 0))],
        out_specs=[pl.BlockSpec((PROJ_ROWS, width), lambda i: (i, 0)),
                   pl.BlockSpec((v_width, PROJ_ROWS), lambda i: (0, i))],
        out_shape=[jax.ShapeDtypeStruct((n, width), jnp.bfloat16),
                   jax.ShapeDtypeStruct((v_width, n), jnp.bfloat16)],
        compiler_params=pltpu.CompilerParams(vmem_limit_bytes=VMEM_LIMIT),
        name="proj",
    )(x2, g_pre, w_in, rope_tab)


def _log_one_minus_beta(nz):
    return jnp.minimum(nz, 0.0) - jnp.log(1.0 + jnp.exp2(jnp.abs(nz) * (-LOG2E)))


def _sb_kernel(q_ref, k_ref, v_ref, out_ref):
    t = ATT_BLOCK
    first_q = pl.program_id(2) * SB_Q_BLOCKS
    lane = lax.broadcasted_iota(jnp.int32, (t, LANES), 1)
    head_lanes = [lane < SB_HEAD_DIM, lane >= SB_HEAD_DIM]
    qs = []
    for j in range(SB_Q_BLOCKS):
        q = q_ref[j * t:(j + 1) * t, :]
        qs.append([jnp.where(m, q, jnp.zeros_like(q)) for m in head_lanes])
    row = lax.broadcasted_iota(jnp.int32, (t, t), 0)
    col = lax.broadcasted_iota(jnp.int32, (t, t), 1)
    tri = (row >= col).astype(jnp.bfloat16)
    tri2 = jnp.concatenate([tri, tri], axis=0)
    strict = col < row

    def run(blocks, carry):
        ks = [k_ref[pl.ds(pl.multiple_of(kb * t, t), t), :] for _, kb, _ in blocks]
        vs = [v_ref[pl.ds(pl.multiple_of(kb * t, t), t), :] for _, kb, _ in blocks]
        nz = [[lax.dot_general(qs[j][h], k, _NT, preferred_element_type=jnp.float32) for h in range(2)]
              for (j, _, _), k in zip(blocks, ks)]
        local = []
        for (_, _, masked), nz_b in zip(blocks, nz):
            parts = []
            for h in range(2):
                lg = _log_one_minus_beta(nz_b[h])
                if masked:
                    lg = jnp.where(strict, lg, 0.0)
                hi = lg.astype(jnp.bfloat16)
                lo = (lg - hi.astype(jnp.float32)).astype(jnp.bfloat16)
                parts.append(jnp.concatenate([hi, lo], axis=1))
            local.append([jnp.dot(hl, tri2, preferred_element_type=jnp.float32) for hl in parts])
        carry = {j: list(c) for j, c in carry.items()}
        ws = []
        for (j, _, masked), nz_b, local_b in zip(blocks, nz, local):
            w_b = []
            for h in range(2):
                c = local_b[h]
                total = jnp.broadcast_to(c[:, :1], (t, LANES))
                if carry[j][h] is not None:
                    c = c + jnp.concatenate([carry[j][h]] * (t // LANES), axis=1)
                    total = total + carry[j][h]
                w = jnp.exp(c - nz_b[h])
                if masked:
                    w = jnp.where(strict, w, 0.0)
                w_b.append(w.astype(jnp.bfloat16))
                carry[j][h] = total
            ws.append(w_b)
        pvs = []
        for w_b, v in zip(ws, vs):
            pvs.append(sum(jnp.dot(w_b[h], jnp.where(head_lanes[h], v, jnp.zeros_like(v)),
                                   preferred_element_type=jnp.float32) for h in range(2)))
        return pvs, carry

    def live(c):
        return (jnp.max(jnp.maximum(c[0], c[1])) > SB_DEAD_LOG).astype(jnp.int32)

    blocks = []
    for j in range(SB_Q_BLOCKS):
        blocks += [(j, first_q + j, True), (j, jnp.maximum(first_q + j - 1, 0), False)]
    pvs, carry = run(blocks, {j: (None, None) for j in range(SB_Q_BLOCKS)})

    for j in range(SB_Q_BLOCKS):
        pv_prev = pvs[2 * j + 1] if j > 0 else jnp.where(first_q > 0, pvs[1], 0.0)

        def step(loop_state, j=j):
            kb, _, acc, c0, c1 = loop_state
            (pv,), c = run([(j, kb, False)], {j: (c0, c1)})
            return kb - 1, live(c[j]), acc + pv, c[j][0], c[j][1]

        state = lax.while_loop(lambda s: (s[0] >= 0) & (s[1] > 0), step,
                               (first_q + j - 2, live(carry[j]), pvs[2 * j] + pv_prev,
                                carry[j][0], carry[j][1]))
        out_ref[j * t:(j + 1) * t, :] = state[2]


def _sb_attn(proj, batch, seq, q_col, k_col, v_col, n_pairs):
    t = ATT_BLOCK * SB_Q_BLOCKS
    nq = seq // t
    return pl.pallas_call(
        _sb_kernel,
        grid=(batch, n_pairs, nq),
        in_specs=[pl.BlockSpec((t, LANES), lambda b, p, i: (b * nq + i, q_col + p)),
                  pl.BlockSpec((seq, LANES), lambda b, p, i: (b, k_col + p)),
                  pl.BlockSpec((seq, LANES), lambda b, p, i: (b, v_col + p))],
        out_specs=pl.BlockSpec((t, LANES), lambda b, p, i: (b * nq + i, p)),
        out_shape=jax.ShapeDtypeStruct((batch * seq, n_pairs * LANES), jnp.float32),
        compiler_params=pltpu.CompilerParams(vmem_limit_bytes=VMEM_LIMIT),
        name="sb_attn",
    )(proj, proj, proj)


def _df_kernel(q_ref, k_ref, vt_ref, lam_ref, g_ref, out_ref, acc_ref, sc0_ref, sc1_ref, mrun_ref, mblk_ref,
               *, lambda_init):
    t = ATT_BLOCK
    sc_refs = (sc0_ref, sc1_ref)
    qi = pl.program_id(1)
    n_pairs = q_ref.shape[1] // LANES
    n_streams = 4 * n_pairs
    lane = lax.broadcasted_iota(jnp.int32, (t, LANES), 1)
    qs = []
    for s in range(n_streams):
        q = q_ref[:, (s // 4) * LANES:(s // 4 + 1) * LANES]
        qs.append(jnp.where(lane // DF_QK_DIM == s % 4, q, jnp.zeros_like(q)))
    key = lax.broadcasted_iota(jnp.int32, (t, t), 0)
    qry = lax.broadcasted_iota(jnp.int32, (t, t), 1)
    allowed = (key // CHUNK) <= (qry // CHUNK)
    ones = jnp.ones((DF_ONES_ROWS, t), jnp.bfloat16)

    def score(kb, slot, masked):
        start = pl.multiple_of(kb * t, t)
        k = k_ref[pl.ds(start, t), :]
        for s in range(n_streams):
            sc = lax.dot_general(k[:, (s // 4) * LANES:(s // 4 + 1) * LANES], qs[s], _NT,
                                 preferred_element_type=jnp.float32)
            if masked:
                sc = jnp.where(allowed, sc, -jnp.inf)
            sc_refs[slot][s] = sc
            mblk_ref[slot, s] = jnp.max(sc, axis=0, keepdims=True)

    def value(kb, slot):
        start = pl.multiple_of(kb * t, t)
        vt = vt_ref[:, pl.ds(start, t)]
        vaug = [jnp.concatenate([vt[h * DF_V_DIM:(h + 1) * DF_V_DIM], ones], axis=0)
                for h in range(n_streams // 2)]
        m_run = [mrun_ref[s] for s in range(n_streams)]
        m_new = [jnp.maximum(m_run[s], mblk_ref[slot, s]) for s in range(n_streams)]
        ps = [jnp.exp2(sc_refs[slot][s] - m_new[s]).astype(jnp.bfloat16) for s in range(n_streams)]
        for s in range(n_streams):
            pv = jnp.dot(vaug[s // 2], ps[s], preferred_element_type=jnp.float32)
            acc_ref[s] = acc_ref[s] * jnp.exp2(m_run[s] - m_new[s]) + pv
            mrun_ref[s] = m_new[s]

    def chain(kb, n):
        for i in range(n):
            score(jnp.maximum(kb - i - 1, 0), (i + 1) % 2, False)
            value(kb - i, i % 2)

    acc_ref[...] = jnp.zeros(acc_ref.shape, jnp.float32)
    mrun_ref[...] = jnp.full(mrun_ref.shape, -jnp.inf, jnp.float32)
    score(qi, 0, True)
    n_blocks = qi + 1
    n_trips = n_blocks // DF_UNROLL

    def trip(p, carry):
        chain(qi - DF_UNROLL * p, DF_UNROLL)
        return carry

    lax.fori_loop(0, n_trips, trip, 0)
    left = n_blocks - DF_UNROLL * n_trips
    size = DF_UNROLL // 2
    while size >= 2:
        @pl.when(left % (2 * size) >= size)
        def _(size=size):
            chain(left % (2 * size) - 1, size)
        size //= 2

    @pl.when(left % 2 == 1)
    def _():
        value(0, 0)

    lp = lam_ref[...]
    lam = (jnp.exp(jnp.sum(lp[0:1] * lp[1:2], axis=1, keepdims=True))
           - jnp.exp(jnp.sum(lp[2:3] * lp[3:4], axis=1, keepdims=True)) + lambda_init)
    heads = []
    for h in range(n_streams // 2):
        o = []
        for m in range(2):
            acc = acc_ref[2 * h + m]
            o.append(acc[:DF_V_DIM] / acc[DF_V_DIM:DF_V_DIM + 1])
        oh = o[0] - lam * o[1]
        ms = jnp.mean(oh * oh, axis=0, keepdims=True)
        heads.append(oh * lax.rsqrt(ms + EPS))
    for p in range(n_pairs):
        y = jnp.concatenate(heads[2 * p:2 * p + 2], axis=0).T
        out_ref[:, p * LANES:(p + 1) * LANES] = (y * g_ref[:, p * LANES:(p + 1) * LANES]
                                                 * (1.0 - lambda_init))


def _df_attn(proj, v_t, df_lambda, g_sub, batch, seq, q_col, k_col, lambda_init):
    t = ATT_BLOCK
    nq = seq // t
    width = v_t.shape[0]
    n_streams = 4 * (width // LANES)
    assert q_col % width == 0 and k_col % width == 0
    return pl.pallas_call(
        functools.partial(_df_kernel, lambda_init=lambda_init),
        grid=(batch, nq),
        in_specs=[pl.BlockSpec((t, width), lambda b, i: (b * nq + i, q_col // width)),
                  pl.BlockSpec((seq, width), lambda b, i: (b, k_col // width)),
                  pl.BlockSpec((width, seq), lambda b, i: (0, b)),
                  pl.BlockSpec(df_lambda.shape, lambda b, i: (0, 0)),
                  pl.BlockSpec((1, width), lambda b, i: (0, 0))],
        out_specs=pl.BlockSpec((t, width), lambda b, i: (b * nq + i, 0)),
        out_shape=jax.ShapeDtypeStruct((batch * seq, width), jnp.float32),
        scratch_shapes=[pltpu.VMEM((n_streams, DF_V_DIM + DF_ONES_ROWS, t), jnp.float32),
                        pltpu.VMEM((n_streams, t, t), jnp.float32),
                        pltpu.VMEM((n_streams, t, t), jnp.float32),
                        pltpu.VMEM((n_streams, 1, t), jnp.float32),
                        pltpu.VMEM((2, n_streams, 1, t), jnp.float32)],
        compiler_params=pltpu.CompilerParams(vmem_limit_bytes=VMEM_LIMIT),
        name="df_attn",
    )(proj, proj, v_t, df_lambda, g_sub)


def _silu(g):
    return g * (1.0 / (1.0 + jnp.exp(-g)))


def _merge_kernel(x_ref, ysb_ref, ydf_ref, gsb_ref, gdf_ref, mq_ref, gm_ref, mkv_ref,
                  w_ref, g_ref, out_ref):
    width = mq_ref.shape[1]
    kv = mkv_ref[0]
    km = kv[:, :width]
    vm = kv[:, width:]
    sub = MERGE_SUB_ROWS
    lane_q = lax.broadcasted_iota(jnp.int32, (sub, width), 1) // MEM_HEAD_DIM
    lane_v = lax.broadcasted_iota(jnp.int32, vm.shape, 1) // MEM_HEAD_DIM
    vms = [jnp.where(lane_v == h, vm, jnp.zeros_like(vm)) for h in range(MEM_HEADS)]
    rows = [pl.ds(i * sub, sub) for i in range(mq_ref.shape[0] // sub)]
    scs = []
    for r in rows:
        mq = mq_ref[r, :]
        scs.append([lax.dot_general(jnp.where(lane_q == h, mq, jnp.zeros_like(mq)), km, _NT,
                                    preferred_element_type=jnp.float32) for h in range(MEM_HEADS)])
    ps = []
    for sc_r in scs:
        p_r = []
        for sc in sc_r:
            p = jnp.exp2(sc - jnp.max(sc, axis=1, keepdims=True))
            p_r.append((p * (1.0 / jnp.sum(p, axis=1, keepdims=True))).astype(jnp.bfloat16))
        ps.append(p_r)
    y_ms = [sum(jnp.dot(p_r[h], vms[h], preferred_element_type=jnp.float32) for h in range(MEM_HEADS))
            for p_r in ps]
    ys = []
    for r, y_m in zip(rows, y_ms):
        ys.append(jnp.concatenate([ysb_ref[r, :] * _silu(gsb_ref[r, :].astype(jnp.float32)),
                                   ydf_ref[r, :] * _silu(gdf_ref[r, :].astype(jnp.float32)),
                                   y_m * _silu(gm_ref[r, :].astype(jnp.float32))], axis=1).astype(jnp.bfloat16))
    os_ = [jnp.dot(y, w_ref[...], preferred_element_type=jnp.float32) for y in ys]
    for r, o in zip(rows, os_):
        out_ref[r, :] = x_ref[r, :] + _rms(o, g_ref[...])


def _merge(x2, y_sb, y_df, proj, mkv, w_out, g_post, seq, cols):
    n, d = x2.shape
    r = MERGE_ROWS
    sb_w, df_w, m_w = y_sb.shape[1], y_df.shape[1], mkv.shape[2] // 2
    gsb_col, gdf_col, mq_col, gm_col = cols
    per_batch = seq // r
    return pl.pallas_call(
        _merge_kernel,
        grid=(n // r,),
        in_specs=[pl.BlockSpec((r, d), lambda i: (i, 0)),
                  pl.BlockSpec((r, sb_w), lambda i: (i, 0)),
                  pl.BlockSpec((r, df_w), lambda i: (i, 0)),
                  pl.BlockSpec((r, sb_w), lambda i: (i, gsb_col // sb_w)),
                  pl.BlockSpec((r, df_w), lambda i: (i, gdf_col // df_w)),
                  pl.BlockSpec((r, m_w), lambda i: (i, mq_col // m_w)),
                  pl.BlockSpec((r, m_w), lambda i: (i, gm_col // m_w)),
                  pl.BlockSpec((1,) + mkv.shape[1:], lambda i: (i // per_batch, 0, 0)),
                  pl.BlockSpec(w_out.shape, lambda i: (0, 0)),
                  pl.BlockSpec((1, d), lambda i: (0, 0))],
        out_specs=pl.BlockSpec((r, d), lambda i: (i, 0)),
        out_shape=jax.ShapeDtypeStruct((n, d), jnp.float32),
        compiler_params=pltpu.CompilerParams(vmem_limit_bytes=VMEM_LIMIT),
        name="merge",
    )(x2, y_sb, y_df, proj, proj, proj, proj, mkv, w_out, g_post)


def _rope_tables(positions):
    inv_freq = 1.0 / (ROPE_THETA ** (jnp.arange(0, DF_ROT_DIMS, 2, dtype=jnp.float32) / DF_ROT_DIMS))
    ang = positions.astype(jnp.float32)[:, :, None] * inv_freq
    b, s = positions.shape
    return jnp.concatenate([jnp.cos(ang), jnp.sin(ang)], axis=-1).reshape(b * s, DF_ROT_DIMS)


def kernel(x, mem, positions, w_in, w_mem_kv, w_out, g_pre, g_post, g_mem, g_subln, df_lambda):
    batch, seq, d = x.shape
    depth = w_in.shape[0]
    sb_w, df_w, m_w = d // 2, d // 4, d // 4
    sb_q, sb_k, sb_v, sb_g = 0, sb_w, 2 * sb_w, 3 * sb_w
    df_q = 4 * sb_w
    df_k, df_v, df_g = df_q + df_w, df_q + 2 * df_w, df_q + 3 * df_w
    m_q, m_g = df_q + 4 * df_w, df_q + 4 * df_w + m_w
    assert m_g + m_w == w_in.shape[2]
    assert seq % (ATT_BLOCK * SB_Q_BLOCKS) == 0 and (batch * seq) % PROJ_ROWS == 0 and seq % MERGE_ROWS == 0

    def layout(col):
        if sb_q <= col < sb_k:
            return -(SB_HEAD_DIM ** -0.5), False
        if df_q <= col < df_k:
            return DF_QK_DIM ** -0.5 * LOG2E, True
        if df_k <= col < df_v:
            return 1.0, True
        if m_q <= col < m_g:
            return MEM_HEAD_DIM ** -0.5 * LOG2E, False
        return 1.0, False

    rope_tab = _rope_tables(positions)
    x2 = x.reshape(batch * seq, d)
    for layer in range(depth):
        lambda_init = 0.8 - 0.6 * math.exp(-0.3 * layer)
        mkv = _mem_kv(mem, g_mem[layer][None], w_mem_kv[layer].astype(jnp.bfloat16))
        proj, v_t = _proj(x2, g_pre[layer][None], w_in[layer].astype(jnp.bfloat16), rope_tab, layout,
                          (df_v, df_g))
        y_sb = _sb_attn(proj, batch, seq, sb_q // LANES, sb_k // LANES, sb_v // LANES, sb_w // LANES)
        g_sub = jnp.tile(g_subln[layer], df_w // DF_V_DIM)[None]
        y_df = _df_attn(proj, v_t, df_lambda[layer], g_sub, batch, seq, df_q, df_k, lambda_init)
        x2 = _merge(x2, y_sb, y_df, proj, mkv, w_out[layer].astype(jnp.bfloat16), g_post[layer][None],
                    seq, (sb_g, df_g, m_q, m_g))
    return x2.reshape(batch, seq, d)
```

```python
import functools
import math

import jax
import jax.numpy as jnp
from jax import lax
from jax.experimental import pallas as pl
from jax.experimental.pallas import tpu as pltpu

EPS = 1e-6
ROPE_THETA = 500000.0
CHUNK = 64

LANES = 128
SB_HEAD_DIM = 64
DF_QK_DIM = 32
DF_V_DIM = 64
DF_ROT_DIMS = 8
MEM_HEAD_DIM = 64
MEM_HEADS = 4
DF_ONES_ROWS = 16
DF_UNROLL = 4
LOG2E = 1.4426950408889634

ATT_BLOCK = 256
SB_Q_BLOCKS = 4
PROJ_ROWS = 512
MERGE_ROWS = 1024
MERGE_SUB_ROWS = 256
PROJ_COL_CHUNK = 512
VMEM_LIMIT = 48 * 1024 * 1024
SB_DEAD_LOG = -110.0

_NT = (((1,), (1,)), ((), ()))


def _rms(xf, g):
    return xf * lax.rsqrt(jnp.mean(xf * xf, axis=-1, keepdims=True) + EPS) * g


def _mem_kv_kernel(mem_ref, g_ref, w_ref, out_ref):
    h = _rms(mem_ref[0], g_ref[...]).astype(jnp.bfloat16)
    out_ref[0] = jnp.dot(h, w_ref[...].astype(jnp.bfloat16),
                         preferred_element_type=jnp.float32).astype(out_ref.dtype)


def _mem_kv(mem, g_mem, w_mem_kv):
    b, n_mem, d = mem.shape
    width = w_mem_kv.shape[1]
    return pl.pallas_call(
        _mem_kv_kernel,
        grid=(b,),
        in_specs=[pl.BlockSpec((1, n_mem, d), lambda i: (i, 0, 0)),
                  pl.BlockSpec((1, d), lambda i: (0, 0)),
                  pl.BlockSpec((d, width), lambda i: (0, 0))],
        out_specs=pl.BlockSpec((1, n_mem, width), lambda i: (i, 0, 0)),
        out_shape=jax.ShapeDtypeStruct((b, n_mem, width), jnp.bfloat16),
        name="mem_kv",
    )(mem, g_mem, w_mem_kv)


def _rope_block(p, cos_t, sin_t):
    lane = lax.broadcasted_iota(jnp.int32, p.shape, 1) % DF_QK_DIM
    half = DF_ROT_DIMS // 2
    partner = jnp.where(lane < half, pltpu.roll(p, LANES - half, 1), pltpu.roll(p, half, 1))
    return p * cos_t + partner * sin_t


def _rope_lanes(cs):
    half = DF_ROT_DIMS // 2
    shape = (cs.shape[0], LANES)
    d = lax.broadcasted_iota(jnp.int32, shape, 1) % DF_QK_DIM
    cos_t = jnp.ones(shape, jnp.float32)
    sin_t = jnp.zeros(shape, jnp.float32)
    for i in range(half):
        c = jnp.broadcast_to(cs[:, i:i + 1], shape)
        s = jnp.broadcast_to(cs[:, half + i:half + i + 1], shape)
        cos_t = jnp.where((d == i) | (d == i + half), c, cos_t)
        sin_t = jnp.where(d == i, -s, jnp.where(d == i + half, s, sin_t))
    return cos_t, sin_t


def _proj_kernel(x_ref, g_ref, w_ref, cs_ref, out_ref, vt_ref, *, layout, vt_cols):
    h = _rms(x_ref[...], g_ref[...]).astype(jnp.bfloat16)
    cos_t, sin_t = _rope_lanes(cs_ref[...])
    n_chunks = w_ref.shape[1] // PROJ_COL_CHUNK
    for c in range(n_chunks):
        c0 = c * PROJ_COL_CHUNK
        p = jnp.dot(h, w_ref[:, c0:c0 + PROJ_COL_CHUNK].astype(jnp.bfloat16),
                    preferred_element_type=jnp.float32)
        for j in range(PROJ_COL_CHUNK // LANES):
            col = c0 + j * LANES
            blk = p[:, j * LANES:(j + 1) * LANES]
            if vt_cols[0] <= col < vt_cols[1]:
                vt_ref[col - vt_cols[0]:col - vt_cols[0] + LANES, :] = blk.T.astype(vt_ref.dtype)
            scale, rope = layout(col)
            if rope:
                blk = _rope_block(blk, cos_t, sin_t)
            if scale != 1.0:
                blk = blk * scale
            out_ref[:, col:col + LANES] = blk.astype(out_ref.dtype)


def _proj(x2, g_pre, w_in, rope_tab, layout, vt_cols):
    n, d = x2.shape
    width = w_in.shape[1]
    v_width = vt_cols[1] - vt_cols[0]
    return pl.pallas_call(
        functools.partial(_proj_kernel, layout=layout, vt_cols=vt_cols),
        grid=(n // PROJ_ROWS,),
        in_specs=[pl.BlockSpec((PROJ_ROWS, d), lambda i: (i, 0)),
                  pl.BlockSpec((1, d), lambda i: (0, 0)),
                  pl.BlockSpec((d, width), lambda i: (0, 0)),
                  pl.BlockSpec((PROJ_ROWS, rope_tab.shape[1]), lambda i: (i, 0))],
        out_specs=[pl.BlockSpec((PROJ_ROWS, width), lambda i: (i, 0)),
                   pl.BlockSpec((v_width, PROJ_ROWS), lambda i: (0, i))],
        out_shape=[jax.ShapeDtypeStruct((n, width), jnp.bfloat16),
                   jax.ShapeDtypeStruct((v_width, n), jnp.bfloat16)],
        compiler_params=pltpu.CompilerParams(vmem_limit_bytes=VMEM_LIMIT),
        name="proj",
    )(x2, g_pre, w_in, rope_tab)


def _log_one_minus_beta(nz):
    return jnp.minimum(nz, 0.0) - jnp.log(1.0 + jnp.exp2(jnp.abs(nz) * (-LOG2E)))


def _sb_kernel(q_ref, k_ref, v_ref, out_ref):
    t = ATT_BLOCK
    first_q = pl.program_id(2) * SB_Q_BLOCKS
    lane = lax.broadcasted_iota(jnp.int32, (t, LANES), 1)
    head_lanes = [lane < SB_HEAD_DIM, lane >= SB_HEAD_DIM]
    qs = []
    for j in range(SB_Q_BLOCKS):
        q = q_ref[j * t:(j + 1) * t, :]
        qs.append([jnp.where(m, q, jnp.zeros_like(q)) for m in head_lanes])
    row = lax.broadcasted_iota(jnp.int32, (t, t), 0)
    col = lax.broadcasted_iota(jnp.int32, (t, t), 1)
    tri = (row >= col).astype(jnp.bfloat16)
    tri2 = jnp.concatenate([tri, tri], axis=0)
    strict = col < row

    def run(blocks, carry):
        ks = [k_ref[pl.ds(pl.multiple_of(kb * t, t), t), :] for _, kb, _ in blocks]
        vs = [v_ref[pl.ds(pl.multiple_of(kb * t, t), t), :] for _, kb, _ in blocks]
        nz = [[lax.dot_general(qs[j][h], k, _NT, preferred_element_type=jnp.float32) for h in range(2)]
              for (j, _, _), k in zip(blocks, ks)]
        local = []
        for (_, _, masked), nz_b in zip(blocks, nz):
            parts = []
            for h in range(2):
                lg = _log_one_minus_beta(nz_b[h])
                if masked:
                    lg = jnp.where(strict, lg, 0.0)
                hi = lg.astype(jnp.bfloat16)
                lo = (lg - hi.astype(jnp.float32)).astype(jnp.bfloat16)
                parts.append(jnp.concatenate([hi, lo], axis=1))
            local.append([jnp.dot(hl, tri2, preferred_element_type=jnp.float32) for hl in parts])
        carry = {j: list(c) for j, c in carry.items()}
        ws = []
        for (j, _, masked), nz_b, local_b in zip(blocks, nz, local):
            w_b = []
            for h in range(2):
                c = local_b[h]
                total = jnp.broadcast_to(c[:, :1], (t, LANES))
                if carry[j][h] is not None:
                    c = c + jnp.concatenate([carry[j][h]] * (t // LANES), axis=1)
                    total = total + carry[j][h]
                w = jnp.exp(c - nz_b[h])
                if masked:
                    w = jnp.where(strict, w, 0.0)
                w_b.append(w.astype(jnp.bfloat16))
                carry[j][h] = total
            ws.append(w_b)
        pvs = []
        for w_b, v in zip(ws, vs):
            pvs.append(sum(jnp.dot(w_b[h], jnp.where(head_lanes[h], v, jnp.zeros_like(v)),
                                   preferred_element_type=jnp.float32) for h in range(2)))
        return pvs, carry

    def live(carry):
        top = functools.reduce(jnp.maximum, [c for cj in carry for c in cj])
        return (jnp.max(top) > SB_DEAD_LOG).astype(jnp.int32)

    nj = SB_Q_BLOCKS
    blocks = []
    for j in range(nj):
        blocks += [(j, first_q + j, True), (j, jnp.maximum(first_q + j - 1, 0), False)]
    pvs, carry = run(blocks, {j: (None, None) for j in range(nj)})
    accs = [pvs[2 * j] + (pvs[2 * j + 1] if j > 0 else jnp.where(first_q > 0, pvs[1], 0.0))
            for j in range(nj)]
    carry = [tuple(carry[j]) for j in range(nj)]

    def step(loop_state):
        kb, _, accs, carry = loop_state
        pvs, new = run([(j, kb, False) for j in range(nj)], dict(enumerate(carry)))
        joined = [kb <= first_q + j - 2 for j in range(nj)]
        accs = [accs[j] + jnp.where(joined[j], pvs[j], 0.0) for j in range(nj)]
        carry = [tuple(jnp.where(joined[j], n, o) for n, o in zip(new[j], carry[j])) for j in range(nj)]
        return kb - 1, live(carry), accs, carry

    state = lax.while_loop(lambda s: (s[0] >= 0) & (s[1] > 0), step,
                           (first_q + nj - 3, live(carry), accs, carry))
    for j in range(nj):
        out_ref[j * t:(j + 1) * t, :] = state[2][j]


def _sb_attn(proj, batch, seq, q_col, k_col, v_col, n_pairs):
    t = ATT_BLOCK * SB_Q_BLOCKS
    nq = seq // t
    return pl.pallas_call(
        _sb_kernel,
        grid=(batch, n_pairs, nq),
        in_specs=[pl.BlockSpec((t, LANES), lambda b, p, i: (b * nq + i, q_col + p)),
                  pl.BlockSpec((seq, LANES), lambda b, p, i: (b, k_col + p)),
                  pl.BlockSpec((seq, LANES), lambda b, p, i: (b, v_col + p))],
        out_specs=pl.BlockSpec((t, LANES), lambda b, p, i: (b * nq + i, p)),
        out_shape=jax.ShapeDtypeStruct((batch * seq, n_pairs * LANES), jnp.float32),
        compiler_params=pltpu.CompilerParams(vmem_limit_bytes=VMEM_LIMIT),
        name="sb_attn",
    )(proj, proj, proj)


def _df_kernel(q_ref, k_ref, vt_ref, lam_ref, g_ref, out_ref, acc_ref, sc0_ref, sc1_ref, mrun_ref, mblk_ref,
               *, lambda_init):
    t = ATT_BLOCK
    sc_refs = (sc0_ref, sc1_ref)
    qi = pl.program_id(1)
    n_pairs = q_ref.shape[1] // LANES
    n_streams = 4 * n_pairs
    lane = lax.broadcasted_iota(jnp.int32, (t, LANES), 1)
    qs = []
    for s in range(n_streams):
        q = q_ref[:, (s // 4) * LANES:(s // 4 + 1) * LANES]
        qs.append(jnp.where(lane // DF_QK_DIM == s % 4, q, jnp.zeros_like(q)))
    key = lax.broadcasted_iota(jnp.int32, (t, t), 0)
    qry = lax.broadcasted_iota(jnp.int32, (t, t), 1)
    allowed = (key // CHUNK) <= (qry // CHUNK)
    ones = jnp.ones((DF_ONES_ROWS, t), jnp.bfloat16)

    def score(kb, slot, masked):
        start = pl.multiple_of(kb * t, t)
        k = k_ref[pl.ds(start, t), :]
        for s in range(n_streams):
            sc = lax.dot_general(k[:, (s // 4) * LANES:(s // 4 + 1) * LANES], qs[s], _NT,
                                 preferred_element_type=jnp.float32)
            if masked:
                sc = jnp.where(allowed, sc, -jnp.inf)
            sc_refs[slot][s] = sc
            mblk_ref[slot, s] = jnp.max(sc, axis=0, keepdims=True)

    def value(kb, slot):
        start = pl.multiple_of(kb * t, t)
        vt = vt_ref[:, pl.ds(start, t)]
        vaug = [jnp.concatenate([vt[h * DF_V_DIM:(h + 1) * DF_V_DIM], ones], axis=0)
                for h in range(n_streams // 2)]
        m_run = [mrun_ref[s] for s in range(n_streams)]
        m_new = [jnp.maximum(m_run[s], mblk_ref[slot, s]) for s in range(n_streams)]
        ps = [jnp.exp2(sc_refs[slot][s] - m_new[s]).astype(jnp.bfloat16) for s in range(n_streams)]
        for s in range(n_streams):
            pv = jnp.dot(vaug[s // 2], ps[s], preferred_element_type=jnp.float32)
            acc_ref[s] = acc_ref[s] * jnp.exp2(m_run[s] - m_new[s]) + pv
            mrun_ref[s] = m_new[s]

    def chain(kb, n):
        for i in range(n):
            score(jnp.maximum(kb - i - 1, 0), (i + 1) % 2, False)
            value(kb - i, i % 2)

    acc_ref[...] = jnp.zeros(acc_ref.shape, jnp.float32)
    mrun_ref[...] = jnp.full(mrun_ref.shape, -jnp.inf, jnp.float32)
    score(qi, 0, True)
    n_blocks = qi + 1
    n_trips = n_blocks // DF_UNROLL

    def trip(p, carry):
        chain(qi - DF_UNROLL * p, DF_UNROLL)
        return carry

    lax.fori_loop(0, n_trips, trip, 0)
    left = n_blocks - DF_UNROLL * n_trips
    size = DF_UNROLL // 2
    while size >= 2:
        @pl.when(left % (2 * size) >= size)
        def _(size=size):
            chain(left % (2 * size) - 1, size)
        size //= 2

    @pl.when(left % 2 == 1)
    def _():
        value(0, 0)

    lp = lam_ref[...]
    lam = (jnp.exp(jnp.sum(lp[0:1] * lp[1:2], axis=1, keepdims=True))
           - jnp.exp(jnp.sum(lp[2:3] * lp[3:4], axis=1, keepdims=True)) + lambda_init)
    heads = []
    for h in range(n_streams // 2):
        o = []
        for m in range(2):
            acc = acc_ref[2 * h + m]
            o.append(acc[:DF_V_DIM] / acc[DF_V_DIM:DF_V_DIM + 1])
        oh = o[0] - lam * o[1]
        ms = jnp.mean(oh * oh, axis=0, keepdims=True)
        heads.append(oh * lax.rsqrt(ms + EPS))
    for p in range(n_pairs):
        y = jnp.concatenate(heads[2 * p:2 * p + 2], axis=0).T
        out_ref[:, p * LANES:(p + 1) * LANES] = (y * g_ref[:, p * LANES:(p + 1) * LANES]
                                                 * (1.0 - lambda_init))


def _df_attn(proj, v_t, df_lambda, g_sub, batch, seq, q_col, k_col, lambda_init):
    t = ATT_BLOCK
    nq = seq // t
    width = v_t.shape[0]
    n_streams = 4 * (width // LANES)
    assert q_col % width == 0 and k_col % width == 0
    return pl.pallas_call(
        functools.partial(_df_kernel, lambda_init=lambda_init),
        grid=(batch, nq),
        in_specs=[pl.BlockSpec((t, width), lambda b, i: (b * nq + i, q_col // width)),
                  pl.BlockSpec((seq, width), lambda b, i: (b, k_col // width)),
                  pl.BlockSpec((width, seq), lambda b, i: (0, b)),
                  pl.BlockSpec(df_lambda.shape, lambda b, i: (0, 0)),
                  pl.BlockSpec((1, width), lambda b, i: (0, 0))],
        out_specs=pl.BlockSpec((t, width), lambda b, i: (b * nq + i, 0)),
        out_shape=jax.ShapeDtypeStruct((batch * seq, width), jnp.float32),
        scratch_shapes=[pltpu.VMEM((n_streams, DF_V_DIM + DF_ONES_ROWS, t), jnp.float32),
                        pltpu.VMEM((n_streams, t, t), jnp.float32),
                        pltpu.VMEM((n_streams, t, t), jnp.float32),
                        pltpu.VMEM((n_streams, 1, t), jnp.float32),
                        pltpu.VMEM((2, n_streams, 1, t), jnp.float32)],
        compiler_params=pltpu.CompilerParams(vmem_limit_bytes=VMEM_LIMIT),
        name="df_attn",
    )(proj, proj, v_t, df_lambda, g_sub)


def _silu(g):
    return g * (1.0 / (1.0 + jnp.exp(-g)))


def _merge_kernel(x_ref, ysb_ref, ydf_ref, gsb_ref, gdf_ref, mq_ref, gm_ref, mkv_ref,
                  w_ref, g_ref, out_ref):
    width = mq_ref.shape[1]
    kv = mkv_ref[0]
    km = kv[:, :width]
    vm = kv[:, width:]
    sub = MERGE_SUB_ROWS
    lane_q = lax.broadcasted_iota(jnp.int32, (sub, width), 1) // MEM_HEAD_DIM
    lane_v = lax.broadcasted_iota(jnp.int32, vm.shape, 1) // MEM_HEAD_DIM
    vms = [jnp.where(lane_v == h, vm, jnp.zeros_like(vm)) for h in range(MEM_HEADS)]
    rows = [pl.ds(i * sub, sub) for i in range(mq_ref.shape[0] // sub)]
    scs = []
    for r in rows:
        mq = mq_ref[r, :]
        scs.append([lax.dot_general(jnp.where(lane_q == h, mq, jnp.zeros_like(mq)), km, _NT,
                                    preferred_element_type=jnp.float32) for h in range(MEM_HEADS)])
    ps = []
    for sc_r in scs:
        p_r = []
        for sc in sc_r:
            p = jnp.exp2(sc - jnp.max(sc, axis=1, keepdims=True))
            p_r.append((p * (1.0 / jnp.sum(p, axis=1, keepdims=True))).astype(jnp.bfloat16))
        ps.append(p_r)
    y_ms = [sum(jnp.dot(p_r[h], vms[h], preferred_element_type=jnp.float32) for h in range(MEM_HEADS))
            for p_r in ps]
    ys = []
    for r, y_m in zip(rows, y_ms):
        ys.append(jnp.concatenate([ysb_ref[r, :] * _silu(gsb_ref[r, :].astype(jnp.float32)),
                                   ydf_ref[r, :] * _silu(gdf_ref[r, :].astype(jnp.float32)),
                                   y_m * _silu(gm_ref[r, :].astype(jnp.float32))], axis=1).astype(jnp.bfloat16))
    w_out = w_ref[...].astype(jnp.bfloat16)
    os_ = [jnp.dot(y, w_out, preferred_element_type=jnp.float32) for y in ys]
    for r, o in zip(rows, os_):
        out_ref[r, :] = x_ref[r, :] + _rms(o, g_ref[...])


def _merge(x2, y_sb, y_df, proj, mkv, w_out, g_post, seq, cols):
    n, d = x2.shape
    r = MERGE_ROWS
    sb_w, df_w, m_w = y_sb.shape[1], y_df.shape[1], mkv.shape[2] // 2
    gsb_col, gdf_col, mq_col, gm_col = cols
    per_batch = seq // r
    return pl.pallas_call(
        _merge_kernel,
        grid=(n // r,),
        in_specs=[pl.BlockSpec((r, d), lambda i: (i, 0)),
                  pl.BlockSpec((r, sb_w), lambda i: (i, 0)),
                  pl.BlockSpec((r, df_w), lambda i: (i, 0)),
                  pl.BlockSpec((r, sb_w), lambda i: (i, gsb_col // sb_w)),
                  pl.BlockSpec((r, df_w), lambda i: (i, gdf_col // df_w)),
                  pl.BlockSpec((r, m_w), lambda i: (i, mq_col // m_w)),
                  pl.BlockSpec((r, m_w), lambda i: (i, gm_col // m_w)),
                  pl.BlockSpec((1,) + mkv.shape[1:], lambda i: (i // per_batch, 0, 0)),
                  pl.BlockSpec(w_out.shape, lambda i: (0, 0)),
                  pl.BlockSpec((1, d), lambda i: (0, 0))],
        out_specs=pl.BlockSpec((r, d), lambda i: (i, 0)),
        out_shape=jax.ShapeDtypeStruct((n, d), jnp.float32),
        compiler_params=pltpu.CompilerParams(vmem_limit_bytes=VMEM_LIMIT),
        name="merge",
    )(x2, y_sb, y_df, proj, proj, proj, proj, mkv, w_out, g_post)


def _rope_tables(positions):
    inv_freq = 1.0 / (ROPE_THETA ** (jnp.arange(0, DF_ROT_DIMS, 2, dtype=jnp.float32) / DF_ROT_DIMS))
    ang = positions.astype(jnp.float32)[:, :, None] * inv_freq
    b, s = positions.shape
    return jnp.concatenate([jnp.cos(ang), jnp.sin(ang)], axis=-1).reshape(b * s, DF_ROT_DIMS)


def kernel(x, mem, positions, w_in, w_mem_kv, w_out, g_pre, g_post, g_mem, g_subln, df_lambda):
    batch, seq, d = x.shape
    depth = w_in.shape[0]
    sb_w, df_w, m_w = d // 2, d // 4, d // 4
    sb_q, sb_k, sb_v, sb_g = 0, sb_w, 2 * sb_w, 3 * sb_w
    df_q = 4 * sb_w
    df_k, df_v, df_g = df_q + df_w, df_q + 2 * df_w, df_q + 3 * df_w
    m_q, m_g = df_q + 4 * df_w, df_q + 4 * df_w + m_w
    assert m_g + m_w == w_in.shape[2]
    assert seq % (ATT_BLOCK * SB_Q_BLOCKS) == 0 and (batch * seq) % PROJ_ROWS == 0 and seq % MERGE_ROWS == 0

    def layout(col):
        if sb_q <= col < sb_k:
            return -(SB_HEAD_DIM ** -0.5), False
        if df_q <= col < df_k:
            return DF_QK_DIM ** -0.5 * LOG2E, True
        if df_k <= col < df_v:
            return 1.0, True
        if m_q <= col < m_g:
            return MEM_HEAD_DIM ** -0.5 * LOG2E, False
        return 1.0, False

    rope_tab = _rope_tables(positions)
    x2 = x.reshape(batch * seq, d)
    for layer in range(depth):
        lambda_init = 0.8 - 0.6 * math.exp(-0.3 * layer)
        mkv = _mem_kv(mem, g_mem[layer][None], w_mem_kv[layer])
        proj, v_t = _proj(x2, g_pre[layer][None], w_in[layer], rope_tab, layout,
                          (df_v, df_g))
        y_sb = _sb_attn(proj, batch, seq, sb_q // LANES, sb_k // LANES, sb_v // LANES, sb_w // LANES)
        g_sub = jnp.tile(g_subln[layer], df_w // DF_V_DIM)[None]
        y_df = _df_attn(proj, v_t, df_lambda[layer], g_sub, batch, seq, df_q, df_k, lambda_init)
        x2 = _merge(x2, y_sb, y_df, proj, mkv, w_out[layer], g_post[layer][None],
                    seq, (sb_g, df_g, m_q, m_g))
    return x2.reshape(batch, seq, d)
```

```python
import functools
import math

import jax
import jax.numpy as jnp
from jax import lax
from jax.experimental import pallas as pl
from jax.experimental.pallas import tpu as pltpu

EPS = 1e-6
ROPE_THETA = 500000.0
CHUNK = 64

LANES = 128
SB_HEAD_DIM = 64
DF_QK_DIM = 32
DF_V_DIM = 64
DF_ROT_DIMS = 8
MEM_HEAD_DIM = 64
MEM_HEADS = 4
DF_ONES_ROWS = 16
DF_UNROLL = 4
LOG2E = 1.4426950408889634

ATT_BLOCK = 256
SB_Q_BLOCKS = 4
PROJ_ROWS = 512
MERGE_ROWS = 1024
MERGE_SUB_ROWS = 256
PROJ_COL_CHUNK = 512
VMEM_LIMIT = 48 * 1024 * 1024
SB_DEAD_LOG = -110.0

_NT = (((1,), (1,)), ((), ()))


def _rms(xf, g):
    return xf * lax.rsqrt(jnp.mean(xf * xf, axis=-1, keepdims=True) + EPS) * g


def _mem_kv_kernel(mem_ref, g_ref, w_ref, out_ref):
    h = _rms(mem_ref[0], g_ref[...]).astype(jnp.bfloat16)
    out_ref[0] = jnp.dot(h, w_ref[...].astype(jnp.bfloat16),
                         preferred_element_type=jnp.float32).astype(out_ref.dtype)


def _mem_kv(mem, g_mem, w_mem_kv):
    b, n_mem, d = mem.shape
    width = w_mem_kv.shape[1]
    return pl.pallas_call(
        _mem_kv_kernel,
        grid=(b,),
        in_specs=[pl.BlockSpec((1, n_mem, d), lambda i: (i, 0, 0)),
                  pl.BlockSpec((1, d), lambda i: (0, 0)),
                  pl.BlockSpec((d, width), lambda i: (0, 0))],
        out_specs=pl.BlockSpec((1, n_mem, width), lambda i: (i, 0, 0)),
        out_shape=jax.ShapeDtypeStruct((b, n_mem, width), jnp.bfloat16),
        name="mem_kv",
    )(mem, g_mem, w_mem_kv)


def _rope_block(p, cos_t, sin_t):
    lane = lax.broadcasted_iota(jnp.int32, p.shape, 1) % DF_QK_DIM
    half = DF_ROT_DIMS // 2
    partner = jnp.where(lane < half, pltpu.roll(p, LANES - half, 1), pltpu.roll(p, half, 1))
    return p * cos_t + partner * sin_t


def _rope_lanes(cs):
    half = DF_ROT_DIMS // 2
    shape = (cs.shape[0], LANES)
    d = lax.broadcasted_iota(jnp.int32, shape, 1) % DF_QK_DIM
    cos_t = jnp.ones(shape, jnp.float32)
    sin_t = jnp.zeros(shape, jnp.float32)
    for i in range(half):
        c = jnp.broadcast_to(cs[:, i:i + 1], shape)
        s = jnp.broadcast_to(cs[:, half + i:half + i + 1], shape)
        cos_t = jnp.where((d == i) | (d == i + half), c, cos_t)
        sin_t = jnp.where(d == i, -s, jnp.where(d == i + half, s, sin_t))
    return cos_t, sin_t


def _proj_kernel(x_ref, g_ref, w_ref, cs_ref, out_ref, vt_ref, *, layout, vt_cols):
    h = _rms(x_ref[...], g_ref[...]).astype(jnp.bfloat16)
    cos_t, sin_t = _rope_lanes(cs_ref[...])
    n_chunks = w_ref.shape[1] // PROJ_COL_CHUNK
    for c in range(n_chunks):
        c0 = c * PROJ_COL_CHUNK
        p = jnp.dot(h, w_ref[:, c0:c0 + PROJ_COL_CHUNK].astype(jnp.bfloat16),
                    preferred_element_type=jnp.float32)
        for j in range(PROJ_COL_CHUNK // LANES):
            col = c0 + j * LANES
            blk = p[:, j * LANES:(j + 1) * LANES]
            if vt_cols[0] <= col < vt_cols[1]:
                vt_ref[col - vt_cols[0]:col - vt_cols[0] + LANES, :] = blk.T.astype(vt_ref.dtype)
            scale, rope = layout(col)
            if rope:
                blk = _rope_block(blk, cos_t, sin_t)
            if scale != 1.0:
                blk = blk * scale
            out_ref[:, col:col + LANES] = blk.astype(out_ref.dtype)


def _proj(x2, g_pre, w_in, rope_tab, layout, vt_cols):
    n, d = x2.shape
    width = w_in.shape[1]
    v_width = vt_cols[1] - vt_cols[0]
    return pl.pallas_call(
        functools.partial(_proj_kernel, layout=layout, vt_cols=vt_cols),
        grid=(n // PROJ_ROWS,),
        in_specs=[pl.BlockSpec((PROJ_ROWS, d), lambda i: (i, 0)),
                  pl.BlockSpec((1, d), lambda i: (0, 0)),
                  pl.BlockSpec((d, width), lambda i: (0, 0)),
                  pl.BlockSpec((PROJ_ROWS, rope_tab.shape[1]), lambda i: (i, 0))],
        out_specs=[pl.BlockSpec((PROJ_ROWS, width), lambda i: (i, 0)),
                   pl.BlockSpec((v_width, PROJ_ROWS), lambda i: (0, i))],
        out_shape=[jax.ShapeDtypeStruct((n, width), jnp.bfloat16),
                   jax.ShapeDtypeStruct((v_width, n), jnp.bfloat16)],
        compiler_params=pltpu.CompilerParams(vmem_limit_bytes=VMEM_LIMIT),
        name="proj",
    )(x2, g_pre, w_in, rope_tab)


def _log_one_minus_beta(nz):
    return jnp.minimum(nz, 0.0) - jnp.log(1.0 + jnp.exp2(jnp.abs(nz) * (-LOG2E)))


def _silu(g):
    return g * (1.0 / (1.0 + jnp.exp(-g)))


def _sb_kernel(q_ref, k_ref, v_ref, gate_ref, out_ref):
    t = ATT_BLOCK
    first_q = pl.program_id(2) * SB_Q_BLOCKS
    lane = lax.broadcasted_iota(jnp.int32, (t, LANES), 1)
    head_lanes = [lane < SB_HEAD_DIM, lane >= SB_HEAD_DIM]
    qs = []
    for j in range(SB_Q_BLOCKS):
        q = q_ref[j * t:(j + 1) * t, :]
        qs.append([jnp.where(m, q, jnp.zeros_like(q)) for m in head_lanes])
    row = lax.broadcasted_iota(jnp.int32, (t, t), 0)
    col = lax.broadcasted_iota(jnp.int32, (t, t), 1)
    tri = (row >= col).astype(jnp.bfloat16)
    tri2 = jnp.concatenate([tri, tri], axis=0)
    strict = col < row

    def run(blocks, carry):
        ks = [k_ref[pl.ds(pl.multiple_of(kb * t, t), t), :] for _, kb, _ in blocks]
        vs = [v_ref[pl.ds(pl.multiple_of(kb * t, t), t), :] for _, kb, _ in blocks]
        nz = [[lax.dot_general(qs[j][h], k, _NT, preferred_element_type=jnp.float32) for h in range(2)]
              for (j, _, _), k in zip(blocks, ks)]
        local = []
        for (_, _, masked), nz_b in zip(blocks, nz):
            parts = []
            for h in range(2):
                lg = _log_one_minus_beta(nz_b[h])
                if masked:
                    lg = jnp.where(strict, lg, 0.0)
                hi = lg.astype(jnp.bfloat16)
                lo = (lg - hi.astype(jnp.float32)).astype(jnp.bfloat16)
                parts.append(jnp.concatenate([hi, lo], axis=1))
            local.append([jnp.dot(hl, tri2, preferred_element_type=jnp.float32) for hl in parts])
        carry = {j: list(c) for j, c in carry.items()}
        ws = []
        for (j, _, masked), nz_b, local_b in zip(blocks, nz, local):
            w_b = []
            for h in range(2):
                c = local_b[h]
                total = jnp.broadcast_to(c[:, :1], (t, LANES))
                if carry[j][h] is not None:
                    c = c + jnp.concatenate([carry[j][h]] * (t // LANES), axis=1)
                    total = total + carry[j][h]
                w = jnp.exp(c - nz_b[h])
                if masked:
                    w = jnp.where(strict, w, 0.0)
                w_b.append(w.astype(jnp.bfloat16))
                carry[j][h] = total
            ws.append(w_b)
        pvs = []
        for w_b, v in zip(ws, vs):
            pvs.append(sum(jnp.dot(w_b[h], jnp.where(head_lanes[h], v, jnp.zeros_like(v)),
                                   preferred_element_type=jnp.float32) for h in range(2)))
        return pvs, carry

    def live(carry):
        top = functools.reduce(jnp.maximum, [c for cj in carry for c in cj])
        return (jnp.max(top) > SB_DEAD_LOG).astype(jnp.int32)

    nj = SB_Q_BLOCKS
    blocks = []
    for j in range(nj):
        blocks += [(j, first_q + j, True), (j, jnp.maximum(first_q + j - 1, 0), False)]
    pvs, carry = run(blocks, {j: (None, None) for j in range(nj)})
    accs = [pvs[2 * j] + (pvs[2 * j + 1] if j > 0 else jnp.where(first_q > 0, pvs[1], 0.0))
            for j in range(nj)]
    carry = [tuple(carry[j]) for j in range(nj)]

    def step(loop_state):
        kb, _, accs, carry = loop_state
        pvs, new = run([(j, kb, False) for j in range(nj)], dict(enumerate(carry)))
        joined = [kb <= first_q + j - 2 for j in range(nj)]
        accs = [accs[j] + jnp.where(joined[j], pvs[j], 0.0) for j in range(nj)]
        carry = [tuple(jnp.where(joined[j], n, o) for n, o in zip(new[j], carry[j])) for j in range(nj)]
        return kb - 1, live(carry), accs, carry

    state = lax.while_loop(lambda s: (s[0] >= 0) & (s[1] > 0), step,
                           (first_q + nj - 3, live(carry), accs, carry))
    for j in range(nj):
        rows = slice(j * t, (j + 1) * t)
        out_ref[rows, :] = (state[2][j] * _silu(gate_ref[rows, :].astype(jnp.float32))).astype(out_ref.dtype)


def _sb_attn(proj, batch, seq, q_col, k_col, v_col, g_col, n_pairs):
    t = ATT_BLOCK * SB_Q_BLOCKS
    nq = seq // t
    return pl.pallas_call(
        _sb_kernel,
        grid=(batch, n_pairs, nq),
        in_specs=[pl.BlockSpec((t, LANES), lambda b, p, i: (b * nq + i, q_col + p)),
                  pl.BlockSpec((seq, LANES), lambda b, p, i: (b, k_col + p)),
                  pl.BlockSpec((seq, LANES), lambda b, p, i: (b, v_col + p)),
                  pl.BlockSpec((t, LANES), lambda b, p, i: (b * nq + i, g_col + p))],
        out_specs=pl.BlockSpec((t, LANES), lambda b, p, i: (b * nq + i, p)),
        out_shape=jax.ShapeDtypeStruct((batch * seq, n_pairs * LANES), jnp.bfloat16),
        compiler_params=pltpu.CompilerParams(vmem_limit_bytes=VMEM_LIMIT),
        name="sb_attn",
    )(proj, proj, proj, proj)


def _df_kernel(q_ref, k_ref, vt_ref, gate_ref, lam_ref, g_ref, out_ref, acc_ref, sc0_ref, sc1_ref, mrun_ref,
               mblk_ref, *, lambda_init):
    t = ATT_BLOCK
    sc_refs = (sc0_ref, sc1_ref)
    qi = pl.program_id(1)
    n_pairs = q_ref.shape[1] // LANES
    n_streams = 4 * n_pairs
    lane = lax.broadcasted_iota(jnp.int32, (t, LANES), 1)
    qs = []
    for s in range(n_streams):
        q = q_ref[:, (s // 4) * LANES:(s // 4 + 1) * LANES]
        qs.append(jnp.where(lane // DF_QK_DIM == s % 4, q, jnp.zeros_like(q)))
    key = lax.broadcasted_iota(jnp.int32, (t, t), 0)
    qry = lax.broadcasted_iota(jnp.int32, (t, t), 1)
    allowed = (key // CHUNK) <= (qry // CHUNK)
    ones = jnp.ones((DF_ONES_ROWS, t), jnp.bfloat16)

    def score(kb, slot, masked):
        start = pl.multiple_of(kb * t, t)
        k = k_ref[pl.ds(start, t), :]
        for s in range(n_streams):
            sc = lax.dot_general(k[:, (s // 4) * LANES:(s // 4 + 1) * LANES], qs[s], _NT,
                                 preferred_element_type=jnp.float32)
            if masked:
                sc = jnp.where(allowed, sc, -jnp.inf)
            sc_refs[slot][s] = sc
            mblk_ref[slot, s] = jnp.max(sc, axis=0, keepdims=True)

    def value(kb, slot):
        start = pl.multiple_of(kb * t, t)
        vt = vt_ref[:, pl.ds(start, t)]
        vaug = [jnp.concatenate([vt[h * DF_V_DIM:(h + 1) * DF_V_DIM], ones], axis=0)
                for h in range(n_streams // 2)]
        m_run = [mrun_ref[s] for s in range(n_streams)]
        m_new = [jnp.maximum(m_run[s], mblk_ref[slot, s]) for s in range(n_streams)]
        ps = [jnp.exp2(sc_refs[slot][s] - m_new[s]).astype(jnp.bfloat16) for s in range(n_streams)]
        for s in range(n_streams):
            pv = jnp.dot(vaug[s // 2], ps[s], preferred_element_type=jnp.float32)
            acc_ref[s] = acc_ref[s] * jnp.exp2(m_run[s] - m_new[s]) + pv
            mrun_ref[s] = m_new[s]

    def chain(kb, n):
        for i in range(n):
            score(jnp.maximum(kb - i - 1, 0), (i + 1) % 2, False)
            value(kb - i, i % 2)

    acc_ref[...] = jnp.zeros(acc_ref.shape, jnp.float32)
    mrun_ref[...] = jnp.full(mrun_ref.shape, -jnp.inf, jnp.float32)
    score(qi, 0, True)
    n_blocks = qi + 1
    n_trips = n_blocks // DF_UNROLL

    def trip(p, carry):
        chain(qi - DF_UNROLL * p, DF_UNROLL)
        return carry

    lax.fori_loop(0, n_trips, trip, 0)
    left = n_blocks - DF_UNROLL * n_trips
    size = DF_UNROLL // 2
    while size >= 2:
        @pl.when(left % (2 * size) >= size)
        def _(size=size):
            chain(left % (2 * size) - 1, size)
        size //= 2

    @pl.when(left % 2 == 1)
    def _():
        value(0, 0)

    lp = lam_ref[...]
    lam = (jnp.exp(jnp.sum(lp[0:1] * lp[1:2], axis=1, keepdims=True))
           - jnp.exp(jnp.sum(lp[2:3] * lp[3:4], axis=1, keepdims=True)) + lambda_init)
    heads = []
    for h in range(n_streams // 2):
        o = []
        for m in range(2):
            acc = acc_ref[2 * h + m]
            o.append(acc[:DF_V_DIM] / acc[DF_V_DIM:DF_V_DIM + 1])
        oh = o[0] - lam * o[1]
        ms = jnp.mean(oh * oh, axis=0, keepdims=True)
        heads.append(oh * lax.rsqrt(ms + EPS))
    for p in range(n_pairs):
        cols = slice(p * LANES, (p + 1) * LANES)
        y = jnp.concatenate(heads[2 * p:2 * p + 2], axis=0).T
        y = y * g_ref[:, cols] * (1.0 - lambda_init)
        out_ref[:, cols] = (y * _silu(gate_ref[:, cols].astype(jnp.float32))).astype(out_ref.dtype)


def _df_attn(proj, v_t, df_lambda, g_sub, batch, seq, q_col, k_col, g_col, lambda_init):
    t = ATT_BLOCK
    nq = seq // t
    width = v_t.shape[0]
    n_streams = 4 * (width // LANES)
    assert q_col % width == 0 and k_col % width == 0 and g_col % width == 0
    return pl.pallas_call(
        functools.partial(_df_kernel, lambda_init=lambda_init),
        grid=(batch, nq),
        in_specs=[pl.BlockSpec((t, width), lambda b, i: (b * nq + i, q_col // width)),
                  pl.BlockSpec((seq, width), lambda b, i: (b, k_col // width)),
                  pl.BlockSpec((width, seq), lambda b, i: (0, b)),
                  pl.BlockSpec((t, width), lambda b, i: (b * nq + i, g_col // width)),
                  pl.BlockSpec(df_lambda.shape, lambda b, i: (0, 0)),
                  pl.BlockSpec((1, width), lambda b, i: (0, 0))],
        out_specs=pl.BlockSpec((t, width), lambda b, i: (b * nq + i, 0)),
        out_shape=jax.ShapeDtypeStruct((batch * seq, width), jnp.bfloat16),
        scratch_shapes=[pltpu.VMEM((n_streams, DF_V_DIM + DF_ONES_ROWS, t), jnp.float32),
                        pltpu.VMEM((n_streams, t, t), jnp.float32),
                        pltpu.VMEM((n_streams, t, t), jnp.float32),
                        pltpu.VMEM((n_streams, 1, t), jnp.float32),
                        pltpu.VMEM((2, n_streams, 1, t), jnp.float32)],
        compiler_params=pltpu.CompilerParams(vmem_limit_bytes=VMEM_LIMIT),
        name="df_attn",
    )(proj, proj, v_t, proj, df_lambda, g_sub)


def _merge_kernel(x_ref, ysb_ref, ydf_ref, mq_ref, gm_ref, mkv_ref, w_ref, g_ref, out_ref):
    width = mq_ref.shape[1]
    kv = mkv_ref[0]
    km = kv[:, :width]
    vm = kv[:, width:]
    sub = MERGE_SUB_ROWS
    lane_q = lax.broadcasted_iota(jnp.int32, (sub, width), 1) // MEM_HEAD_DIM
    lane_v = lax.broadcasted_iota(jnp.int32, vm.shape, 1) // MEM_HEAD_DIM
    vms = [jnp.where(lane_v == h, vm, jnp.zeros_like(vm)) for h in range(MEM_HEADS)]
    rows = [pl.ds(i * sub, sub) for i in range(mq_ref.shape[0] // sub)]
    scs = []
    for r in rows:
        mq = mq_ref[r, :]
        scs.append([lax.dot_general(jnp.where(lane_q == h, mq, jnp.zeros_like(mq)), km, _NT,
                                    preferred_element_type=jnp.float32) for h in range(MEM_HEADS)])
    ps = []
    for sc_r in scs:
        p_r = []
        for sc in sc_r:
            p = jnp.exp2(sc - jnp.max(sc, axis=1, keepdims=True))
            p_r.append((p * (1.0 / jnp.sum(p, axis=1, keepdims=True))).astype(jnp.bfloat16))
        ps.append(p_r)
    y_ms = [sum(jnp.dot(p_r[h], vms[h], preferred_element_type=jnp.float32) for h in range(MEM_HEADS))
            for p_r in ps]
    ys = []
    for r, y_m in zip(rows, y_ms):
        gated_m = (y_m * _silu(gm_ref[r, :].astype(jnp.float32))).astype(jnp.bfloat16)
        ys.append(jnp.concatenate([ysb_ref[r, :], ydf_ref[r, :], gated_m], axis=1))
    w_out = w_ref[...].astype(jnp.bfloat16)
    os_ = [jnp.dot(y, w_out, preferred_element_type=jnp.float32) for y in ys]
    for r, o in zip(rows, os_):
        out_ref[r, :] = x_ref[r, :] + _rms(o, g_ref[...])


def _merge(x2, y_sb, y_df, proj, mkv, w_out, g_post, seq, cols):
    n, d = x2.shape
    r = MERGE_ROWS
    sb_w, df_w, m_w = y_sb.shape[1], y_df.shape[1], mkv.shape[2] // 2
    mq_col, gm_col = cols
    per_batch = seq // r
    return pl.pallas_call(
        _merge_kernel,
        grid=(n // r,),
        in_specs=[pl.BlockSpec((r, d), lambda i: (i, 0)),
                  pl.BlockSpec((r, sb_w), lambda i: (i, 0)),
                  pl.BlockSpec((r, df_w), lambda i: (i, 0)),
                  pl.BlockSpec((r, m_w), lambda i: (i, mq_col // m_w)),
                  pl.BlockSpec((r, m_w), lambda i: (i, gm_col // m_w)),
                  pl.BlockSpec((1,) + mkv.shape[1:], lambda i: (i // per_batch, 0, 0)),
                  pl.BlockSpec(w_out.shape, lambda i: (0, 0)),
                  pl.BlockSpec((1, d), lambda i: (0, 0))],
        out_specs=pl.BlockSpec((r, d), lambda i: (i, 0)),
        out_shape=jax.ShapeDtypeStruct((n, d), jnp.float32),
        compiler_params=pltpu.CompilerParams(vmem_limit_bytes=VMEM_LIMIT),
        name="merge",
    )(x2, y_sb, y_df, proj, proj, mkv, w_out, g_post)


def _rope_tables(positions):
    inv_freq = 1.0 / (ROPE_THETA ** (jnp.arange(0, DF_ROT_DIMS, 2, dtype=jnp.float32) / DF_ROT_DIMS))
    ang = positions.astype(jnp.float32)[:, :, None] * inv_freq
    b, s = positions.shape
    return jnp.concatenate([jnp.cos(ang), jnp.sin(ang)], axis=-1).reshape(b * s, DF_ROT_DIMS)


def kernel(x, mem, positions, w_in, w_mem_kv, w_out, g_pre, g_post, g_mem, g_subln, df_lambda):
    batch, seq, d = x.shape
    depth = w_in.shape[0]
    sb_w, df_w, m_w = d // 2, d // 4, d // 4
    sb_q, sb_k, sb_v, sb_g = 0, sb_w, 2 * sb_w, 3 * sb_w
    df_q = 4 * sb_w
    df_k, df_v, df_g = df_q + df_w, df_q + 2 * df_w, df_q + 3 * df_w
    m_q, m_g = df_q + 4 * df_w, df_q + 4 * df_w + m_w
    assert m_g + m_w == w_in.shape[2]
    assert seq % (ATT_BLOCK * SB_Q_BLOCKS) == 0 and (batch * seq) % PROJ_ROWS == 0 and seq % MERGE_ROWS == 0

    def layout(col):
        if sb_q <= col < sb_k:
            return -(SB_HEAD_DIM ** -0.5), False
        if df_q <= col < df_k:
            return DF_QK_DIM ** -0.5 * LOG2E, True
        if df_k <= col < df_v:
            return 1.0, True
        if m_q <= col < m_g:
            return MEM_HEAD_DIM ** -0.5 * LOG2E, False
        return 1.0, False

    rope_tab = _rope_tables(positions)
    x2 = x.reshape(batch * seq, d)
    for layer in range(depth):
        lambda_init = 0.8 - 0.6 * math.exp(-0.3 * layer)
        mkv = _mem_kv(mem, g_mem[layer][None], w_mem_kv[layer])
        proj, v_t = _proj(x2, g_pre[layer][None], w_in[layer], rope_tab, layout,
                          (df_v, df_g))
        y_sb = _sb_attn(proj, batch, seq, sb_q // LANES, sb_k // LANES, sb_v // LANES, sb_g // LANES,
                        sb_w // LANES)
        g_sub = jnp.tile(g_subln[layer], df_w // DF_V_DIM)[None]
        y_df = _df_attn(proj, v_t, df_lambda[layer], g_sub, batch, seq, df_q, df_k, df_g, lambda_init)
        x2 = _merge(x2, y_sb, y_df, proj, mkv, w_out[layer], g_post[layer][None], seq, (m_q, m_g))
    return x2.reshape(batch, seq, d)
```

```python
import functools
import math

import jax
import jax.numpy as jnp
from jax import lax
from jax.experimental import pallas as pl
from jax.experimental.pallas import tpu as pltpu

EPS = 1e-6
ROPE_THETA = 500000.0
CHUNK = 64

LANES = 128
SB_HEAD_DIM = 64
DF_QK_DIM = 32
DF_V_DIM = 64
DF_ROT_DIMS = 8
MEM_HEAD_DIM = 64
MEM_HEADS = 4
DF_ONES_ROWS = 16
DF_UNROLL = 4
LOG2E = 1.4426950408889634

ATT_BLOCK = 256
SB_Q_BLOCKS = 4
PROJ_ROWS = 512
MERGE_ROWS = 1024
MERGE_SUB_ROWS = 256
PROJ_COL_CHUNK = 512
VMEM_LIMIT = 48 * 1024 * 1024
SB_DEAD_LOG = -110.0

_NT = (((1,), (1,)), ((), ()))


def _rms(xf, g):
    return xf * lax.rsqrt(jnp.mean(xf * xf, axis=-1, keepdims=True) + EPS) * g


def _mem_kv_kernel(mem_ref, g_ref, w_ref, out_ref):
    h = _rms(mem_ref[0], g_ref[...]).astype(jnp.bfloat16)
    out_ref[0] = jnp.dot(h, w_ref[...].astype(jnp.bfloat16),
                         preferred_element_type=jnp.float32).astype(out_ref.dtype)


def _mem_kv(mem, g_mem, w_mem_kv):
    b, n_mem, d = mem.shape
    width = w_mem_kv.shape[1]
    return pl.pallas_call(
        _mem_kv_kernel,
        grid=(b,),
        in_specs=[pl.BlockSpec((1, n_mem, d), lambda i: (i, 0, 0)),
                  pl.BlockSpec((1, d), lambda i: (0, 0)),
                  pl.BlockSpec((d, width), lambda i: (0, 0))],
        out_specs=pl.BlockSpec((1, n_mem, width), lambda i: (i, 0, 0)),
        out_shape=jax.ShapeDtypeStruct((b, n_mem, width), jnp.bfloat16),
        name="mem_kv",
    )(mem, g_mem, w_mem_kv)


def _rope_block(p, cos_t, sin_t):
    lane = lax.broadcasted_iota(jnp.int32, p.shape, 1) % DF_QK_DIM
    half = DF_ROT_DIMS // 2
    partner = jnp.where(lane < half, pltpu.roll(p, LANES - half, 1), pltpu.roll(p, half, 1))
    return p * cos_t + partner * sin_t


def _rope_lanes(cs_t):
    pad = jnp.zeros((LANES - cs_t.shape[0], cs_t.shape[1]), jnp.float32)
    cs = jnp.concatenate([cs_t, pad], axis=0).T
    half = DF_ROT_DIMS // 2
    shape = (cs.shape[0], LANES)
    d = lax.broadcasted_iota(jnp.int32, shape, 1) % DF_QK_DIM
    cos_t = jnp.ones(shape, jnp.float32)
    sin_t = jnp.zeros(shape, jnp.float32)
    for i in range(half):
        c = jnp.broadcast_to(cs[:, i:i + 1], shape)
        s = jnp.broadcast_to(cs[:, half + i:half + i + 1], shape)
        cos_t = jnp.where((d == i) | (d == i + half), c, cos_t)
        sin_t = jnp.where(d == i, -s, jnp.where(d == i + half, s, sin_t))
    return cos_t, sin_t


def _proj_kernel(x_ref, g_ref, w_ref, cs_ref, out_ref, vt_ref, *, layout, vt_cols):
    h = _rms(x_ref[...], g_ref[...]).astype(jnp.bfloat16)
    cos_t, sin_t = _rope_lanes(cs_ref[...])
    n_chunks = w_ref.shape[1] // PROJ_COL_CHUNK
    for c in range(n_chunks):
        c0 = c * PROJ_COL_CHUNK
        p = jnp.dot(h, w_ref[:, c0:c0 + PROJ_COL_CHUNK].astype(jnp.bfloat16),
                    preferred_element_type=jnp.float32)
        for j in range(PROJ_COL_CHUNK // LANES):
            col = c0 + j * LANES
            blk = p[:, j * LANES:(j + 1) * LANES]
            if vt_cols[0] <= col < vt_cols[1]:
                vt_ref[col - vt_cols[0]:col - vt_cols[0] + LANES, :] = blk.T.astype(vt_ref.dtype)
            scale, rope = layout(col)
            if rope:
                blk = _rope_block(blk, cos_t, sin_t)
            if scale != 1.0:
                blk = blk * scale
            out_ref[:, col:col + LANES] = blk.astype(out_ref.dtype)


def _proj(x2, g_pre, w_in, rope_tab, layout, vt_cols):
    n, d = x2.shape
    width = w_in.shape[1]
    v_width = vt_cols[1] - vt_cols[0]
    return pl.pallas_call(
        functools.partial(_proj_kernel, layout=layout, vt_cols=vt_cols),
        grid=(n // PROJ_ROWS,),
        in_specs=[pl.BlockSpec((PROJ_ROWS, d), lambda i: (i, 0)),
                  pl.BlockSpec((1, d), lambda i: (0, 0)),
                  pl.BlockSpec((d, width), lambda i: (0, 0)),
                  pl.BlockSpec((rope_tab.shape[0], PROJ_ROWS), lambda i: (0, i))],
        out_specs=[pl.BlockSpec((PROJ_ROWS, width), lambda i: (i, 0)),
                   pl.BlockSpec((v_width, PROJ_ROWS), lambda i: (0, i))],
        out_shape=[jax.ShapeDtypeStruct((n, width), jnp.bfloat16),
                   jax.ShapeDtypeStruct((v_width, n), jnp.bfloat16)],
        compiler_params=pltpu.CompilerParams(vmem_limit_bytes=VMEM_LIMIT),
        name="proj",
    )(x2, g_pre, w_in, rope_tab)


def _log_one_minus_beta(nz):
    return jnp.minimum(nz, 0.0) - jnp.log(1.0 + jnp.exp2(jnp.abs(nz) * (-LOG2E)))


def _silu(g):
    return g * (1.0 / (1.0 + jnp.exp(-g)))


def _sb_kernel(q_ref, k_ref, v_ref, gate_ref, out_ref):
    t = ATT_BLOCK
    first_q = pl.program_id(2) * SB_Q_BLOCKS
    lane = lax.broadcasted_iota(jnp.int32, (t, LANES), 1)
    head_lanes = [lane < SB_HEAD_DIM, lane >= SB_HEAD_DIM]
    qs = []
    for j in range(SB_Q_BLOCKS):
        q = q_ref[j * t:(j + 1) * t, :]
        qs.append([jnp.where(m, q, jnp.zeros_like(q)) for m in head_lanes])
    row = lax.broadcasted_iota(jnp.int32, (t, t), 0)
    col = lax.broadcasted_iota(jnp.int32, (t, t), 1)
    tri = (row >= col).astype(jnp.bfloat16)
    tri2 = jnp.concatenate([tri, tri], axis=0)
    strict = col < row

    def run(blocks, carry):
        ks = [k_ref[pl.ds(pl.multiple_of(kb * t, t), t), :] for _, kb, _ in blocks]
        vs = [v_ref[pl.ds(pl.multiple_of(kb * t, t), t), :] for _, kb, _ in blocks]
        nz = [[lax.dot_general(qs[j][h], k, _NT, preferred_element_type=jnp.float32) for h in range(2)]
              for (j, _, _), k in zip(blocks, ks)]
        local = []
        for (_, _, masked), nz_b in zip(blocks, nz):
            parts = []
            for h in range(2):
                lg = _log_one_minus_beta(nz_b[h])
                if masked:
                    lg = jnp.where(strict, lg, 0.0)
                hi = lg.astype(jnp.bfloat16)
                lo = (lg - hi.astype(jnp.float32)).astype(jnp.bfloat16)
                parts.append(jnp.concatenate([hi, lo], axis=1))
            local.append([jnp.dot(hl, tri2, preferred_element_type=jnp.float32) for hl in parts])
        carry = {j: list(c) for j, c in carry.items()}
        ws = []
        for (j, _, masked), nz_b, local_b in zip(blocks, nz, local):
            w_b = []
            for h in range(2):
                c = local_b[h]
                total = jnp.broadcast_to(c[:, :1], (t, LANES))
                if carry[j][h] is not None:
                    c = c + jnp.concatenate([carry[j][h]] * (t // LANES), axis=1)
                    total = total + carry[j][h]
                w = jnp.exp(c - nz_b[h])
                if masked:
                    w = jnp.where(strict, w, 0.0)
                w_b.append(w.astype(jnp.bfloat16))
                carry[j][h] = total
            ws.append(w_b)
        pvs = []
        for w_b, v in zip(ws, vs):
            pvs.append(sum(jnp.dot(w_b[h], jnp.where(head_lanes[h], v, jnp.zeros_like(v)),
                                   preferred_element_type=jnp.float32) for h in range(2)))
        return pvs, carry

    def live(carry):
        top = functools.reduce(jnp.maximum, [c for cj in carry for c in cj])
        return (jnp.max(top) > SB_DEAD_LOG).astype(jnp.int32)

    nj = SB_Q_BLOCKS
    blocks = []
    for j in range(nj):
        blocks += [(j, first_q + j, True), (j, jnp.maximum(first_q + j - 1, 0), False)]
    pvs, carry = run(blocks, {j: (None, None) for j in range(nj)})
    accs = [pvs[2 * j] + (pvs[2 * j + 1] if j > 0 else jnp.where(first_q > 0, pvs[1], 0.0))
            for j in range(nj)]
    carry = [tuple(carry[j]) for j in range(nj)]

    def step(loop_state):
        kb, _, accs, carry = loop_state
        pvs, new = run([(j, kb, False) for j in range(nj)], dict(enumerate(carry)))
        joined = [kb <= first_q + j - 2 for j in range(nj)]
        accs = [accs[j] + jnp.where(joined[j], pvs[j], 0.0) for j in range(nj)]
        carry = [tuple(jnp.where(joined[j], n, o) for n, o in zip(new[j], carry[j])) for j in range(nj)]
        return kb - 1, live(carry), accs, carry

    state = lax.while_loop(lambda s: (s[0] >= 0) & (s[1] > 0), step,
                           (first_q + nj - 3, live(carry), accs, carry))
    for j in range(nj):
        rows = slice(j * t, (j + 1) * t)
        out_ref[rows, :] = (state[2][j] * _silu(gate_ref[rows, :].astype(jnp.float32))).astype(out_ref.dtype)


def _sb_attn(proj, batch, seq, q_col, k_col, v_col, g_col, n_pairs):
    t = ATT_BLOCK * SB_Q_BLOCKS
    nq = seq // t
    return pl.pallas_call(
        _sb_kernel,
        grid=(batch, n_pairs, nq),
        in_specs=[pl.BlockSpec((t, LANES), lambda b, p, i: (b * nq + i, q_col + p)),
                  pl.BlockSpec((seq, LANES), lambda b, p, i: (b, k_col + p)),
                  pl.BlockSpec((seq, LANES), lambda b, p, i: (b, v_col + p)),
                  pl.BlockSpec((t, LANES), lambda b, p, i: (b * nq + i, g_col + p))],
        out_specs=pl.BlockSpec((t, LANES), lambda b, p, i: (b * nq + i, p)),
        out_shape=jax.ShapeDtypeStruct((batch * seq, n_pairs * LANES), jnp.bfloat16),
        compiler_params=pltpu.CompilerParams(vmem_limit_bytes=VMEM_LIMIT),
        name="sb_attn",
    )(proj, proj, proj, proj)


def _df_kernel(q_ref, k_ref, vt_ref, gate_ref, lam_ref, g_ref, out_ref, acc_ref, sc0_ref, sc1_ref, mrun_ref,
               mblk_ref, *, lambda_init):
    t = ATT_BLOCK
    sc_refs = (sc0_ref, sc1_ref)
    qi = pl.program_id(1)
    n_pairs = q_ref.shape[1] // LANES
    n_streams = 4 * n_pairs
    lane = lax.broadcasted_iota(jnp.int32, (t, LANES), 1)
    qs = []
    for s in range(n_streams):
        q = q_ref[:, (s // 4) * LANES:(s // 4 + 1) * LANES]
        qs.append(jnp.where(lane // DF_QK_DIM == s % 4, q, jnp.zeros_like(q)))
    key = lax.broadcasted_iota(jnp.int32, (t, t), 0)
    qry = lax.broadcasted_iota(jnp.int32, (t, t), 1)
    allowed = (key // CHUNK) <= (qry // CHUNK)
    ones = jnp.ones((DF_ONES_ROWS, t), jnp.bfloat16)

    def score(kb, slot, masked):
        start = pl.multiple_of(kb * t, t)
        k = k_ref[pl.ds(start, t), :]
        for s in range(n_streams):
            sc = lax.dot_general(k[:, (s // 4) * LANES:(s // 4 + 1) * LANES], qs[s], _NT,
                                 preferred_element_type=jnp.float32)
            if masked:
                sc = jnp.where(allowed, sc, -jnp.inf)
            sc_refs[slot][s] = sc
            mblk_ref[slot, s] = jnp.max(sc, axis=0, keepdims=True)

    def value(kb, slot):
        start = pl.multiple_of(kb * t, t)
        vt = vt_ref[:, pl.ds(start, t)]
        vaug = [jnp.concatenate([vt[h * DF_V_DIM:(h + 1) * DF_V_DIM], ones], axis=0)
                for h in range(n_streams // 2)]
        m_run = [mrun_ref[s] for s in range(n_streams)]
        m_new = [jnp.maximum(m_run[s], mblk_ref[slot, s]) for s in range(n_streams)]
        ps = [jnp.exp2(sc_refs[slot][s] - m_new[s]).astype(jnp.bfloat16) for s in range(n_streams)]
        for s in range(n_streams):
            pv = jnp.dot(vaug[s // 2], ps[s], preferred_element_type=jnp.float32)
            acc_ref[s] = acc_ref[s] * jnp.exp2(m_run[s] - m_new[s]) + pv
            mrun_ref[s] = m_new[s]

    def chain(kb, n):
        for i in range(n):
            score(jnp.maximum(kb - i - 1, 0), (i + 1) % 2, False)
            value(kb - i, i % 2)

    acc_ref[...] = jnp.zeros(acc_ref.shape, jnp.float32)
    mrun_ref[...] = jnp.full(mrun_ref.shape, -jnp.inf, jnp.float32)
    score(qi, 0, True)
    n_blocks = qi + 1
    n_trips = n_blocks // DF_UNROLL

    def trip(p, carry):
        chain(qi - DF_UNROLL * p, DF_UNROLL)
        return carry

    lax.fori_loop(0, n_trips, trip, 0)
    left = n_blocks - DF_UNROLL * n_trips
    size = DF_UNROLL // 2
    while size >= 2:
        @pl.when(left % (2 * size) >= size)
        def _(size=size):
            chain(left % (2 * size) - 1, size)
        size //= 2

    @pl.when(left % 2 == 1)
    def _():
        value(0, 0)

    lp = lam_ref[...]
    lam = (jnp.exp(jnp.sum(lp[0:1] * lp[1:2], axis=1, keepdims=True))
           - jnp.exp(jnp.sum(lp[2:3] * lp[3:4], axis=1, keepdims=True)) + lambda_init)
    heads = []
    for h in range(n_streams // 2):
        o = []
        for m in range(2):
            acc = acc_ref[2 * h + m]
            o.append(acc[:DF_V_DIM] / acc[DF_V_DIM:DF_V_DIM + 1])
        oh = o[0] - lam * o[1]
        ms = jnp.mean(oh * oh, axis=0, keepdims=True)
        heads.append(oh * lax.rsqrt(ms + EPS))
    for p in range(n_pairs):
        cols = slice(p * LANES, (p + 1) * LANES)
        y = jnp.concatenate(heads[2 * p:2 * p + 2], axis=0).T
        y = y * g_ref[:, cols] * (1.0 - lambda_init)
        out_ref[:, cols] = (y * _silu(gate_ref[:, cols].astype(jnp.float32))).astype(out_ref.dtype)


def _df_attn(proj, v_t, df_lambda, g_sub, batch, seq, q_col, k_col, g_col, lambda_init):
    t = ATT_BLOCK
    nq = seq // t
    width = v_t.shape[0]
    n_streams = 4 * (width // LANES)
    assert q_col % width == 0 and k_col % width == 0 and g_col % width == 0
    return pl.pallas_call(
        functools.partial(_df_kernel, lambda_init=lambda_init),
        grid=(batch, nq),
        in_specs=[pl.BlockSpec((t, width), lambda b, i: (b * nq + i, q_col // width)),
                  pl.BlockSpec((seq, width), lambda b, i: (b, k_col // width)),
                  pl.BlockSpec((width, seq), lambda b, i: (0, b)),
                  pl.BlockSpec((t, width), lambda b, i: (b * nq + i, g_col // width)),
                  pl.BlockSpec(df_lambda.shape, lambda b, i: (0, 0)),
                  pl.BlockSpec((1, width), lambda b, i: (0, 0))],
        out_specs=pl.BlockSpec((t, width), lambda b, i: (b * nq + i, 0)),
        out_shape=jax.ShapeDtypeStruct((batch * seq, width), jnp.bfloat16),
        scratch_shapes=[pltpu.VMEM((n_streams, DF_V_DIM + DF_ONES_ROWS, t), jnp.float32),
                        pltpu.VMEM((n_streams, t, t), jnp.float32),
                        pltpu.VMEM((n_streams, t, t), jnp.float32),
                        pltpu.VMEM((n_streams, 1, t), jnp.float32),
                        pltpu.VMEM((2, n_streams, 1, t), jnp.float32)],
        compiler_params=pltpu.CompilerParams(vmem_limit_bytes=VMEM_LIMIT),
        name="df_attn",
    )(proj, proj, v_t, proj, df_lambda, g_sub)


def _merge_kernel(x_ref, ysb_ref, ydf_ref, mq_ref, gm_ref, mkv_ref, w_ref, g_ref, out_ref):
    width = mq_ref.shape[1]
    kv = mkv_ref[0]
    km = kv[:, :width]
    vm = kv[:, width:]
    sub = MERGE_SUB_ROWS
    lane_q = lax.broadcasted_iota(jnp.int32, (sub, width), 1) // MEM_HEAD_DIM
    lane_v = lax.broadcasted_iota(jnp.int32, vm.shape, 1) // MEM_HEAD_DIM
    vms = [jnp.where(lane_v == h, vm, jnp.zeros_like(vm)) for h in range(MEM_HEADS)]
    rows = [pl.ds(i * sub, sub) for i in range(mq_ref.shape[0] // sub)]
    scs = []
    for r in rows:
        mq = mq_ref[r, :]
        scs.append([lax.dot_general(jnp.where(lane_q == h, mq, jnp.zeros_like(mq)), km, _NT,
                                    preferred_element_type=jnp.float32) for h in range(MEM_HEADS)])
    ps = []
    for sc_r in scs:
        p_r = []
        for sc in sc_r:
            p = jnp.exp2(sc - jnp.max(sc, axis=1, keepdims=True))
            p_r.append((p * (1.0 / jnp.sum(p, axis=1, keepdims=True))).astype(jnp.bfloat16))
        ps.append(p_r)
    y_ms = [sum(jnp.dot(p_r[h], vms[h], preferred_element_type=jnp.float32) for h in range(MEM_HEADS))
            for p_r in ps]
    ys = []
    for r, y_m in zip(rows, y_ms):
        gated_m = (y_m * _silu(gm_ref[r, :].astype(jnp.float32))).astype(jnp.bfloat16)
        ys.append(jnp.concatenate([ysb_ref[r, :], ydf_ref[r, :], gated_m], axis=1))
    w_out = w_ref[...].astype(jnp.bfloat16)
    os_ = [jnp.dot(y, w_out, preferred_element_type=jnp.float32) for y in ys]
    for r, o in zip(rows, os_):
        out_ref[r, :] = x_ref[r, :] + _rms(o, g_ref[...])


def _merge(x2, y_sb, y_df, proj, mkv, w_out, g_post, seq, cols):
    n, d = x2.shape
    r = MERGE_ROWS
    sb_w, df_w, m_w = y_sb.shape[1], y_df.shape[1], mkv.shape[2] // 2
    mq_col, gm_col = cols
    per_batch = seq // r
    return pl.pallas_call(
        _merge_kernel,
        grid=(n // r,),
        in_specs=[pl.BlockSpec((r, d), lambda i: (i, 0)),
                  pl.BlockSpec((r, sb_w), lambda i: (i, 0)),
                  pl.BlockSpec((r, df_w), lambda i: (i, 0)),
                  pl.BlockSpec((r, m_w), lambda i: (i, mq_col // m_w)),
                  pl.BlockSpec((r, m_w), lambda i: (i, gm_col // m_w)),
                  pl.BlockSpec((1,) + mkv.shape[1:], lambda i: (i // per_batch, 0, 0)),
                  pl.BlockSpec(w_out.shape, lambda i: (0, 0)),
                  pl.BlockSpec((1, d), lambda i: (0, 0))],
        out_specs=pl.BlockSpec((r, d), lambda i: (i, 0)),
        out_shape=jax.ShapeDtypeStruct((n, d), jnp.float32),
        compiler_params=pltpu.CompilerParams(vmem_limit_bytes=VMEM_LIMIT),
        name="merge",
    )(x2, y_sb, y_df, proj, proj, mkv, w_out, g_post)


def _rope_tables(positions):
    inv_freq = 1.0 / (ROPE_THETA ** (jnp.arange(0, DF_ROT_DIMS, 2, dtype=jnp.float32) / DF_ROT_DIMS))
    ang = inv_freq[:, None] * positions.astype(jnp.float32).reshape(1, -1)
    return jnp.concatenate([jnp.cos(ang), jnp.sin(ang)], axis=0)


def kernel(x, mem, positions, w_in, w_mem_kv, w_out, g_pre, g_post, g_mem, g_subln, df_lambda):
    batch, seq, d = x.shape
    depth = w_in.shape[0]
    sb_w, df_w, m_w = d // 2, d // 4, d // 4
    sb_q, sb_k, sb_v, sb_g = 0, sb_w, 2 * sb_w, 3 * sb_w
    df_q = 4 * sb_w
    df_k, df_v, df_g = df_q + df_w, df_q + 2 * df_w, df_q + 3 * df_w
    m_q, m_g = df_q + 4 * df_w, df_q + 4 * df_w + m_w
    assert m_g + m_w == w_in.shape[2]
    assert seq % (ATT_BLOCK * SB_Q_BLOCKS) == 0 and (batch * seq) % PROJ_ROWS == 0 and seq % MERGE_ROWS == 0

    def layout(col):
        if sb_q <= col < sb_k:
            return -(SB_HEAD_DIM ** -0.5), False
        if df_q <= col < df_k:
            return DF_QK_DIM ** -0.5 * LOG2E, True
        if df_k <= col < df_v:
            return 1.0, True
        if m_q <= col < m_g:
            return MEM_HEAD_DIM ** -0.5 * LOG2E, False
        return 1.0, False

    rope_tab = _rope_tables(positions)
    x2 = x.reshape(batch * seq, d)
    for layer in range(depth):
        lambda_init = 0.8 - 0.6 * math.exp(-0.3 * layer)
        mkv = _mem_kv(mem, g_mem[layer][None], w_mem_kv[layer])
        proj, v_t = _proj(x2, g_pre[layer][None], w_in[layer], rope_tab, layout,
                          (df_v, df_g))
        y_sb = _sb_attn(proj, batch, seq, sb_q // LANES, sb_k // LANES, sb_v // LANES, sb_g // LANES,
                        sb_w // LANES)
        g_sub = jnp.tile(g_subln[layer], df_w // DF_V_DIM)[None]
        y_df = _df_attn(proj, v_t, df_lambda[layer], g_sub, batch, seq, df_q, df_k, df_g, lambda_init)
        x2 = _merge(x2, y_sb, y_df, proj, mkv, w_out[layer], g_post[layer][None], seq, (m_q, m_g))
    return x2.reshape(batch, seq, d)
```
